```python
import jax, jax.numpy as jnp
from jax import lax
import numpy as np

D_MODEL = 1024
BATCH = 8
SEQ = 2048
DEPTH = 2

MIX_WIDTH = D_MODEL
SGU_WIDTH = MIX_WIDTH // 2
SGU_GROUPS = 4
SGU_GROUP_DIM = SGU_WIDTH // SGU_GROUPS
CHUNK = 128
SB_WIDTH = MIX_WIDTH - SGU_WIDTH
SB_HEADS = 8
SB_HEAD_DIM = SB_WIDTH // SB_HEADS
Q_BLOCK = 128
IN_COLS = 2 * SGU_WIDTH + 3 * SB_WIDTH
MEM_LEN = 256
MEM_HEADS = 4
MEM_HEAD_DIM = D_MODEL // MEM_HEADS
N_EXPERTS = 32
TOP_K = 4
D_EXPERT = D_MODEL
SWIGLU_LIMIT = 7.0
SWIGLU_ALPHA = 1.702
MOE_BLOCK = 128
DEEPNORM_ALPHA = (2 * DEPTH) ** 0.25
DEEPNORM_BETA = (8 * DEPTH) ** -0.25
LN_EPS = 1e-5
RMS_EPS = 1e-6

kernel_name = 'hybrid_sgu_stickbreak_mem_moe_deepnorm'


def layer_norm(x, g, b):
    xf = x.astype(jnp.float32)
    mu = jnp.mean(xf, axis=-1, keepdims=True)
    var = jnp.mean(jnp.square(xf - mu), axis=-1, keepdims=True)
    y = (xf - mu) * lax.rsqrt(var + LN_EPS)
    return (y * g.astype(jnp.float32) + b.astype(jnp.float32)).astype(x.dtype)


def group_layer_norm(x, g, b, groups):
    shp = x.shape
    xf = x.astype(jnp.float32).reshape(shp[:-1] + (groups, shp[-1] // groups))
    mu = jnp.mean(xf, axis=-1, keepdims=True)
    var = jnp.mean(jnp.square(xf - mu), axis=-1, keepdims=True)
    y = ((xf - mu) * lax.rsqrt(var + LN_EPS)).reshape(shp)
    return (y * g.astype(jnp.float32) + b.astype(jnp.float32)).astype(x.dtype)


def group_rms(x, groups):
    shp = x.shape
    xf = x.astype(jnp.float32).reshape(shp[:-1] + (groups, shp[-1] // groups))
    y = xf * lax.rsqrt(jnp.mean(jnp.square(xf), axis=-1, keepdims=True) + RMS_EPS)
    return y.reshape(shp).astype(x.dtype)


def stick_breaking_attention(q, k, v):
    S = q.shape[2]
    scale = SB_HEAD_DIM ** -0.5
    qf = q.astype(jnp.float32)
    kf = k.astype(jnp.float32)
    outs = []
    for i in range(S // Q_BLOCK):
        lo, hi = i * Q_BLOCK, (i + 1) * Q_BLOCK
        z = jnp.einsum('bhtd,bhsd->bhts', qf[:, :, lo:hi], kf[:, :, :hi]) * scale
        t_pos = lo + jnp.arange(Q_BLOCK)[:, None]
        s_pos = jnp.arange(hi)[None, :]
        mask = s_pos < t_pos
        log_not = jnp.where(mask, jax.nn.log_sigmoid(-z), 0.0)
        log_w = jax.nn.log_sigmoid(z) + lax.cumsum(log_not, axis=3, reverse=True) - log_not
        a = jnp.where(mask, jnp.exp(log_w), 0.0)
        outs.append(jnp.einsum('bhts,bhsd->bhtd', a.astype(v.dtype), v[:, :, :hi]))
    return jnp.concatenate(outs, axis=2)


def parallel_mixer(h, w_in, sgu_g, sgu_b, w_sp, b_sp, grp_g, w_out):
    B, S, _ = h.shape
    proj = h @ w_in
    u, v_s, q, k, v = jnp.split(
        proj, [SGU_WIDTH, 2 * SGU_WIDTH, 2 * SGU_WIDTH + SB_WIDTH, 2 * SGU_WIDTH + 2 * SB_WIDTH], axis=-1)
    u = jax.nn.gelu(u)
    v_s = group_layer_norm(jax.nn.gelu(v_s), sgu_g, sgu_b, SGU_GROUPS)
    vc = v_s.reshape(B, S // CHUNK, CHUNK, SGU_GROUPS, SGU_GROUP_DIM)
    causal = jnp.tril(jnp.ones((CHUNK, CHUNK), dtype=bool))
    w_c = jnp.where(causal[None], w_sp, 0.0).astype(vc.dtype)
    sg = jnp.einsum('gts,bcsgd->bctgd', w_c, vc) + b_sp.T[:, :, None]
    out_a = u * sg.reshape(B, S, SGU_WIDTH)
    to_heads = lambda t: t.reshape(B, S, SB_HEADS, SB_HEAD_DIM).transpose(0, 2, 1, 3)
    out_b = stick_breaking_attention(to_heads(q), to_heads(k), to_heads(v))
    out_b = out_b.transpose(0, 2, 1, 3).reshape(B, S, SB_WIDTH)
    y = jnp.concatenate([group_rms(out_a, SGU_GROUPS), group_rms(out_b, SB_HEADS)], axis=-1) * grp_g
    return y @ w_out


def memory_attention(x, mem, wq, wkv, wo):
    B, S, D = x.shape
    M = mem.shape[1]
    q = (x @ wq).reshape(B, S, MEM_HEADS, MEM_HEAD_DIM)
    k, v = jnp.split(mem @ wkv, 2, axis=-1)
    k = k.reshape(B, M, MEM_HEADS, MEM_HEAD_DIM)
    v = v.reshape(B, M, MEM_HEADS, MEM_HEAD_DIM)
    s = jnp.einsum('bshd,bmhd->bhsm', q.astype(jnp.float32), k.astype(jnp.float32)) * (MEM_HEAD_DIM ** -0.5)
    p = jax.nn.softmax(s, axis=-1).astype(x.dtype)
    o = jnp.einsum('bhsm,bmhd->bshd', p, v).reshape(B, S, D)
    return o @ wo


def moe(x, w_router, b_router, w_gu, b_gu, w_down, b_down):
    B, S, D = x.shape
    N = B * S
    NK = N * TOP_K
    x2 = x.reshape(N, D)
    logits = (x2 @ w_router + b_router).astype(jnp.float32)
    top_val, top_idx = lax.top_k(logits, TOP_K)
    gates = jax.nn.softmax(top_val, axis=-1).astype(x.dtype)
    flat_e = top_idx.reshape(NK).astype(jnp.int32)
    flat_tok = jnp.arange(NK, dtype=jnp.int32) // TOP_K
    order = jnp.argsort(flat_e)
    e_sorted = flat_e[order]
    tok_sorted = flat_tok[order]
    gate_sorted = gates.reshape(NK)[order]
    counts = jnp.bincount(flat_e, length=N_EXPERTS)
    starts = jnp.cumsum(counts) - counts
    padded = ((counts + MOE_BLOCK - 1) // MOE_BLOCK) * MOE_BLOCK
    pend = jnp.cumsum(padded)
    pstart = pend - padded
    dest = pstart[e_sorted] + jnp.arange(NK, dtype=jnp.int32) - starts[e_sorted]
    P = ((NK + MOE_BLOCK - 1) // MOE_BLOCK) * MOE_BLOCK + N_EXPERTS * MOE_BLOCK
    n_blocks = P // MOE_BLOCK
    row_tok = jnp.full((P,), N, dtype=jnp.int32).at[dest].set(tok_sorted)
    x_pad = jnp.concatenate([x2, jnp.zeros((1, D), x2.dtype)], axis=0)
    rows = x_pad[row_tok].reshape(n_blocks, MOE_BLOCK, D)
    block_e = jnp.searchsorted(pend, jnp.arange(n_blocks) * MOE_BLOCK, side='right')
    block_e = jnp.clip(block_e, 0, N_EXPERTS - 1)

    def expert_block(args):
        xb, e = args
        hgu = xb @ w_gu[e] + b_gu[e]
        g, u = hgu[:, :D_EXPERT], hgu[:, D_EXPERT:]
        g = jnp.minimum(g, SWIGLU_LIMIT)
        u = jnp.clip(u, -SWIGLU_LIMIT, SWIGLU_LIMIT)
        a = (u + 1.0) * (g * jax.nn.sigmoid(SWIGLU_ALPHA * g))
        return a @ w_down[e] + b_down[e]

    y = lax.map(expert_block, (rows, block_e)).reshape(P, D)
    y_sel = y[dest] * gate_sorted[:, None]
    out = jax.ops.segment_sum(y_sel, tok_sorted, num_segments=N)
    return out.reshape(B, S, D)


def setup_inputs(seed: int = 0) -> dict:
    key = jax.random.key(seed)
    ks = jax.random.split(key, 24)
    f32 = jnp.float32
    nrm = lambda k, shp, s: jax.random.normal(k, shp, f32) * s
    L, D, E, F = DEPTH, D_MODEL, N_EXPERTS, D_EXPERT
    return {
        'x': nrm(ks[0], (BATCH, SEQ, D), 1.0),
        'mem': nrm(ks[1], (BATCH, MEM_LEN, D), 1.0),
        'w_in': nrm(ks[2], (L, D, IN_COLS), D ** -0.5),
        'sgu_g': 1.0 + nrm(ks[3], (L, SGU_WIDTH), 0.1),
        'sgu_b': nrm(ks[4], (L, SGU_WIDTH), 0.02),
        'w_sp': nrm(ks[5], (L, SGU_GROUPS, CHUNK, CHUNK), CHUNK ** -0.5),
        'b_sp': 1.0 + nrm(ks[6], (L, SGU_GROUPS, CHUNK), 0.1),
        'grp_g': 1.0 + nrm(ks[7], (L, MIX_WIDTH), 0.1),
        'w_out': nrm(ks[8], (L, MIX_WIDTH, D), DEEPNORM_BETA * MIX_WIDTH ** -0.5),
        'wq_mem': nrm(ks[9], (L, D, D), D ** -0.5),
        'wkv_mem': nrm(ks[10], (L, D, 2 * D), D ** -0.5),
        'wo_mem': nrm(ks[11], (L, D, D), DEEPNORM_BETA * D ** -0.5),
        'w_router': nrm(ks[12], (L, D, E), D ** -0.5),
        'b_router': nrm(ks[13], (L, E), 0.01),
        'w_gu': nrm(ks[14], (L, E, D, 2 * F), D ** -0.5),
        'b_gu': nrm(ks[15], (L, E, 2 * F), 0.02),
        'w_down': nrm(ks[16], (L, E, F, D), DEEPNORM_BETA * F ** -0.5),
        'b_down': nrm(ks[17], (L, E, D), 0.02),
        'ln_g': 1.0 + nrm(ks[18], (L, 3, D), 0.1),
        'ln_b': nrm(ks[19], (L, 3, D), 0.02),
    }


def reference(x, mem, w_in, sgu_g, sgu_b, w_sp, b_sp, grp_g, w_out, wq_mem, wkv_mem, wo_mem,
              w_router, b_router, w_gu, b_gu, w_down, b_down, ln_g, ln_b):
    for l in range(DEPTH):
        mix = parallel_mixer(x, w_in[l], sgu_g[l], sgu_b[l], w_sp[l], b_sp[l], grp_g[l], w_out[l])
        x = layer_norm(DEEPNORM_ALPHA * x + mix, ln_g[l, 0], ln_b[l, 0])
        xa = memory_attention(x, mem, wq_mem[l], wkv_mem[l], wo_mem[l])
        x = layer_norm(DEEPNORM_ALPHA * x + xa, ln_g[l, 1], ln_b[l, 1])
        ff = moe(x, w_router[l], b_router[l], w_gu[l], b_gu[l], w_down[l], b_down[l])
        x = layer_norm(DEEPNORM_ALPHA * x + ff, ln_g[l, 2], ln_b[l, 2])
    return x
```

```python
import jax
import jax.numpy as jnp
from jax import lax
from jax.experimental import pallas as pl
from jax.experimental.pallas import tpu as pltpu

F32 = jnp.float32
BF16 = jnp.bfloat16

D_MODEL = 1024
SGU_WIDTH = 512
SGU_GROUPS = 4
GROUP_DIM = 128
CHUNK = 128
SB_WIDTH = 512
SB_HEADS = 8
SB_HEAD_DIM = 64
MEM_HEADS = 4
MEM_HEAD_DIM = 256
N_EXPERTS = 32
TOP_K = 4
SWIGLU_LIMIT = 7.0
SWIGLU_ALPHA = 1.702
LN_EPS = 1e-5
RMS_EPS = 1e-6

LANES = 128
TM_MIX = 512
TQ_SB = 256
TM_POST = 512
TD_ROWS = 256
BLK_E = 256
VMEM_LIMIT = 56 * 1024 * 1024


def _layer_norm(x, g, b):
    mu = jnp.mean(x, axis=-1, keepdims=True)
    xc = x - mu
    var = jnp.mean(xc * xc, axis=-1, keepdims=True)
    return xc * lax.rsqrt(var + LN_EPS) * g + b


def _dot(a, b):
    return jnp.dot(a, b, preferred_element_type=F32)


def _dot_nt(a, b):
    return lax.dot_general(a, b, (((1,), (1,)), ((), ())), preferred_element_type=F32)


def _mixer_in_kernel(x_ref, win_ref, sg_ref, sb_ref, wsp_ref, bsp_ref, gg_ref,
                     ya_ref, q_ref, k_ref, v_ref):
    xb = x_ref[...].astype(BF16)

    def proj(lo, hi):
        return _dot(xb, win_ref[:, lo:hi])

    u = jax.nn.gelu(proj(0, SGU_WIDTH))
    vs = jax.nn.gelu(proj(SGU_WIDTH, 2 * SGU_WIDTH))
    r = lax.broadcasted_iota(jnp.int32, (CHUNK, CHUNK), 0)
    c = lax.broadcasted_iota(jnp.int32, (CHUNK, CHUNK), 1)
    causal = r >= c
    for g in range(SGU_GROUPS):
        gs = slice(g * GROUP_DIM, (g + 1) * GROUP_DIM)
        vn = _layer_norm(vs[:, gs], sg_ref[:, gs], sb_ref[:, gs]).astype(BF16)
        wc = jnp.where(causal, wsp_ref[g], 0.0).astype(BF16)
        for ch in range(TM_MIX // CHUNK):
            cs = slice(ch * CHUNK, (ch + 1) * CHUNK)
            gate = _dot(wc, vn[cs]) + bsp_ref[g]
            oa = u[cs, gs] * gate
            ms = jnp.mean(oa * oa, axis=-1, keepdims=True)
            ya_ref[cs, gs] = (oa * lax.rsqrt(ms + RMS_EPS) * gg_ref[:, gs]).astype(BF16)
    o = 2 * SGU_WIDTH
    q_ref[...] = (proj(o, o + SB_WIDTH) * (SB_HEAD_DIM ** -0.5)).astype(BF16)
    k_ref[...] = proj(o + SB_WIDTH, o + 2 * SB_WIDTH).astype(BF16)
    v_ref[...] = proj(o + 2 * SB_WIDTH, o + 3 * SB_WIDTH).astype(BF16)


def _mixer_in(x, w_in, sgu_g, sgu_b, w_sp, b_sp, grp_g_a):
    B, S, D = x.shape
    full = lambda shape: pl.BlockSpec(shape, lambda b, i: (0,) * len(shape))
    tile = lambda w: pl.BlockSpec((None, TM_MIX, w), lambda b, i: (b, i, 0))
    out = jax.ShapeDtypeStruct((B, S, SGU_WIDTH), BF16)
    return pl.pallas_call(
        _mixer_in_kernel,
        out_shape=(out, out, out, out),
        grid=(B, S // TM_MIX),
        in_specs=[tile(D), full(w_in.shape), full(sgu_g.shape), full(sgu_b.shape),
                  full(w_sp.shape), full(b_sp.shape), full(grp_g_a.shape)],
        out_specs=(tile(SGU_WIDTH),) * 4,
        compiler_params=pltpu.CompilerParams(
            dimension_semantics=("parallel", "parallel"), vmem_limit_bytes=VMEM_LIMIT),
        name="mixer_in",
    )(x, w_in, sgu_g, sgu_b, w_sp, b_sp, grp_g_a)


def _sb_attn_kernel(q_ref, k_ref, v_ref, gg_ref, o_ref):
    i = pl.program_id(2)
    lo = i * TQ_SB
    lane = lax.broadcasted_iota(jnp.int32, (1, LANES), 1)
    head0 = lane < SB_HEAD_DIM
    q = q_ref[...]
    zq = jnp.zeros_like(q)
    qs = jnp.concatenate([jnp.where(head0, q, zq), jnp.where(head0, zq, q)], axis=0)
    r = lax.broadcasted_iota(jnp.int32, (LANES, LANES), 0)
    c = lax.broadcasted_iota(jnp.int32, (LANES, LANES), 1)
    after = jnp.where(r > c, 1.0, 0.0).astype(BF16)
    t_pos = lo + lax.rem(lax.broadcasted_iota(jnp.int32, (2 * TQ_SB, LANES), 0), TQ_SB)
    s_off = lax.broadcasted_iota(jnp.int32, (2 * TQ_SB, LANES), 1)

    def key_block(s0, carry, acc, masked):
        s0 = pl.multiple_of(s0, LANES)
        kb = k_ref[pl.ds(s0, LANES), :]
        vb = v_ref[pl.ds(s0, LANES), :]
        z = _dot_nt(qs, kb)
        l1p = jnp.log1p(jnp.exp(-jnp.abs(z)))
        ls_pos = jnp.minimum(z, 0.0) - l1p
        log_not = ls_pos - z
        if masked:
            mask = (s0 + s_off) < t_pos
            log_not = jnp.where(mask, log_not, 0.0)
        hi = log_not.astype(BF16)
        lo_part = (log_not - hi.astype(F32)).astype(BF16)
        later = _dot(hi, after) + _dot(lo_part, after)
        a = jnp.exp(ls_pos + later + carry)
        if masked:
            a = jnp.where(mask, a, 0.0)
        ab = a.astype(BF16)
        a2 = jnp.concatenate([ab[:TQ_SB], ab[TQ_SB:]], axis=1)
        zv = jnp.zeros_like(vb)
        v2 = jnp.concatenate([jnp.where(head0, vb, zv), jnp.where(head0, zv, vb)], axis=0)
        acc = acc + _dot(a2, v2)
        carry = carry + jnp.sum(log_not, axis=1, keepdims=True)
        return carry, acc

    carry = jnp.zeros((2 * TQ_SB, LANES), F32)
    acc = jnp.zeros((TQ_SB, LANES), F32)
    for sb in reversed(range(TQ_SB // LANES)):
        carry, acc = key_block(lo + sb * LANES, carry, acc, True)

    def body(j, st):
        s0 = pl.multiple_of(lo - (j + 1) * TQ_SB, TQ_SB)
        cr, ac = st
        for sb in reversed(range(TQ_SB // LANES)):
            cr, ac = key_block(s0 + sb * LANES, cr, ac, False)
        return cr, ac

    carry, acc = lax.fori_loop(0, i, body, (carry, acc))

    sq = acc * acc
    ss0 = jnp.sum(jnp.where(head0, sq, 0.0), axis=1, keepdims=True)
    ss1 = jnp.sum(jnp.where(head0, 0.0, sq), axis=1, keepdims=True)
    inv = 1.0 / SB_HEAD_DIM
    rs = jnp.where(head0, lax.rsqrt(ss0 * inv + RMS_EPS), lax.rsqrt(ss1 * inv + RMS_EPS))
    o_ref[...] = (acc * rs * gg_ref[...]).astype(BF16)


def _sb_attn(q, k, v, grp_g_b):
    B, S, W = q.shape
    pairs = W // LANES
    return pl.pallas_call(
        _sb_attn_kernel,
        out_shape=jax.ShapeDtypeStruct((B, S, W), BF16),
        grid=(B, pairs, S // TQ_SB),
        in_specs=[pl.BlockSpec((None, TQ_SB, LANES), lambda b, p, i: (b, i, p)),
                  pl.BlockSpec((None, S, LANES), lambda b, p, i: (b, 0, p)),
                  pl.BlockSpec((None, S, LANES), lambda b, p, i: (b, 0, p)),
                  pl.BlockSpec((1, LANES), lambda b, p, i: (0, p))],
        out_specs=pl.BlockSpec((None, TQ_SB, LANES), lambda b, p, i: (b, i, p)),
        compiler_params=pltpu.CompilerParams(
            dimension_semantics=("parallel", "parallel", "arbitrary"),
            vmem_limit_bytes=VMEM_LIMIT),
        name="sb_attn",
    )(q, k, v, grp_g_b)


def _kv_proj_kernel(mem_ref, wkv_ref, k_ref, v_ref):
    mb = mem_ref[...].astype(BF16)
    k_ref[...] = _dot(mb, wkv_ref[:, :D_MODEL]).astype(BF16)
    v_ref[...] = _dot(mb, wkv_ref[:, D_MODEL:]).astype(BF16)


def _kv_proj(mem, wkv):
    B, M, D = mem.shape
    out = jax.ShapeDtypeStruct((B, M, D), BF16)
    return pl.pallas_call(
        _kv_proj_kernel,
        out_shape=(out, out),
        grid=(B,),
        in_specs=[pl.BlockSpec((None, M, D), lambda b: (b, 0, 0)),
                  pl.BlockSpec(wkv.shape, lambda b: (0, 0))],
        out_specs=(pl.BlockSpec((None, M, D), lambda b: (b, 0, 0)),) * 2,
        compiler_params=pltpu.CompilerParams(
            dimension_semantics=("parallel",), vmem_limit_bytes=VMEM_LIMIT),
        name="kv_proj",
    )(mem, wkv)


def _pack_bf16_pairs(x):
    w = x.shape[1] // 2
    bits = lax.bitcast_convert_type(x.astype(BF16).astype(F32), jnp.uint32)
    return (bits[:, w:] & jnp.uint32(0xFFFF0000)) | (bits[:, :w] >> 16)


def _unpack_bf16_pairs(u):
    lo = lax.bitcast_convert_type(u << 16, F32).astype(BF16)
    hi = lax.bitcast_convert_type(u & jnp.uint32(0xFFFF0000), F32).astype(BF16)
    return lo, hi


def _post_mixer_kernel(x_ref, ya_ref, yb_ref, wout_ref, lng_ref, lnb_ref, wq_ref, km_ref, vm_ref,
                       wo_ref, wr_ref, br_ref,
                       x2_ref, x2p_ref, idx_ref, rank_ref, gate_ref, cnt_ref, carry_ref):
    first = jnp.logical_and(pl.program_id(0) == 0, pl.program_id(1) == 0)

    @pl.when(first)
    def _():
        carry_ref[...] = jnp.zeros_like(carry_ref)

    alpha = (2 * 2) ** 0.25
    mix = _dot(ya_ref[...], wout_ref[:SGU_WIDTH, :]) + _dot(yb_ref[...], wout_ref[SGU_WIDTH:, :])
    x1 = _layer_norm(alpha * x_ref[...] + mix, lng_ref[0:1, :], lnb_ref[0:1, :])

    qm = (_dot(x1.astype(BF16), wq_ref[...]) * (MEM_HEAD_DIM ** -0.5)).astype(BF16)
    heads = []
    for h in range(MEM_HEADS):
        hs = slice(h * MEM_HEAD_DIM, (h + 1) * MEM_HEAD_DIM)
        s = _dot_nt(qm[:, hs], km_ref[:, hs])
        e = jnp.exp(s - jnp.max(s, axis=-1, keepdims=True))
        p = (e / jnp.sum(e, axis=-1, keepdims=True)).astype(BF16)
        heads.append(_dot(p, vm_ref[:, hs]).astype(BF16))
    o = jnp.concatenate(heads, axis=1)
    x2 = _layer_norm(alpha * x1 + _dot(o, wo_ref[...]), lng_ref[1:2, :], lnb_ref[1:2, :])
    x2_ref[...] = x2
    x2p_ref[...] = _pack_bf16_pairs(x2)

    xh = x2.astype(BF16)
    xl = (x2 - xh.astype(F32)).astype(BF16)
    wr = wr_ref[...]
    wh = wr.astype(BF16)
    wl = (wr - wh.astype(F32)).astype(BF16)
    logits = _dot(xh, wh) + (_dot(xl, wh) + _dot(xh, wl)) + br_ref[...]

    tm = logits.shape[0]
    lane = lax.broadcasted_iota(jnp.int32, (tm, N_EXPERTS), 1).astype(F32)
    work = logits
    vals, idxs = [], []
    chosen = jnp.zeros((tm, N_EXPERTS), F32)
    for _ in range(TOP_K):
        m = jnp.max(work, axis=-1, keepdims=True)
        ix = jnp.min(jnp.where(work == m, lane, float(N_EXPERTS)), axis=-1, keepdims=True)
        sel = lane == ix
        work = jnp.where(sel, -jnp.inf, work)
        chosen = jnp.where(sel, 1.0, chosen)
        vals.append(m)
        idxs.append(ix)
    ev = [jnp.exp(v - vals[0]) for v in vals]
    den = ev[0] + ev[1] + ev[2] + ev[3]

    r = lax.broadcasted_iota(jnp.int32, (tm, tm), 0)
    c = lax.broadcasted_iota(jnp.int32, (tm, tm), 1)
    before = jnp.where(r > c, 1.0, 0.0).astype(BF16)
    running = _dot(before, chosen.astype(BF16)) + carry_ref[...]
    ranks = [jnp.sum(jnp.where(lane == ix, running, 0.0), axis=-1, keepdims=True) for ix in idxs]
    carry_ref[...] = carry_ref[...] + jnp.sum(chosen, axis=0, keepdims=True)
    cnt_ref[...] = carry_ref[...]

    l4 = lax.broadcasted_iota(jnp.int32, (tm, TOP_K), 1)

    def cols(xs):
        return jnp.where(l4 == 0, xs[0], jnp.where(l4 == 1, xs[1], jnp.where(l4 == 2, xs[2], xs[3])))

    idx_ref[...] = cols(idxs).astype(jnp.int32)
    rank_ref[...] = cols(ranks).astype(jnp.int32)
    gate_ref[...] = cols([e / den for e in ev])


def _post_mixer(x, ya, yb, w_out, ln_g, ln_b, wq, k_mem, v_mem, wo, w_router, b_router):
    B, S, D = x.shape
    M = k_mem.shape[1]
    full = lambda a: pl.BlockSpec(a.shape, lambda b, i: (0,) * a.ndim)
    tile = lambda w: pl.BlockSpec((None, TM_POST, w), lambda b, i: (b, i, 0))
    memb = pl.BlockSpec((None, M, D), lambda b, i: (b, 0, 0))
    sds = jax.ShapeDtypeStruct
    return pl.pallas_call(
        _post_mixer_kernel,
        out_shape=(sds((B, S, D), F32), sds((B, S, D // 2), jnp.uint32),
                   sds((B, S, TOP_K), jnp.int32), sds((B, S, TOP_K), jnp.int32),
                   sds((B, S, TOP_K), F32), sds((1, N_EXPERTS), F32)),
        grid=(B, S // TM_POST),
        in_specs=[tile(D), tile(SGU_WIDTH), tile(SB_WIDTH), full(w_out), full(ln_g), full(ln_b),
                  full(wq), memb, memb, full(wo), full(w_router), full(b_router)],
        out_specs=(tile(D), tile(D // 2), tile(TOP_K), tile(TOP_K), tile(TOP_K),
                   pl.BlockSpec((1, N_EXPERTS), lambda b, i: (0, 0))),
        scratch_shapes=[pltpu.VMEM((1, N_EXPERTS), F32)],
        compiler_params=pltpu.CompilerParams(
            dimension_semantics=("arbitrary", "arbitrary"), vmem_limit_bytes=VMEM_LIMIT),
        name="post_mixer",
    )(x, ya, yb, w_out, ln_g, ln_b, wq, k_mem, v_mem, wo, w_router, b_router)


def _row_copy(src_ref, src_row, dst_ref, dst_row, sem):
    return pltpu.make_async_copy(src_ref.at[pl.ds(src_row, 1)], dst_ref.at[pl.ds(dst_row, 1)], sem)


def _dispatch_kernel(dest_ref, x2p_ref, rows_in_ref, rows_ref, sem):
    del rows_in_ref

    def start(t, carry):
        for k in range(TOP_K):
            _row_copy(x2p_ref, t, rows_ref, dest_ref[t * TOP_K + k], sem).start()
        return carry

    lax.fori_loop(0, TD_ROWS, start, 0)

    def wait(t, carry):
        for k in range(TOP_K):
            _row_copy(x2p_ref, 0, rows_ref, 0, sem).wait()
        return carry

    lax.fori_loop(0, TD_ROWS, wait, 0)


def _dispatch(dest_flat, x2p, rows_init):
    N, W = x2p.shape
    return pl.pallas_call(
        _dispatch_kernel,
        out_shape=jax.ShapeDtypeStruct(rows_init.shape, rows_init.dtype),
        grid=(N // TD_ROWS,),
        in_specs=[pl.BlockSpec((TD_ROWS * TOP_K,), lambda i: (i,), memory_space=pltpu.SMEM),
                  pl.BlockSpec((TD_ROWS, W), lambda i: (i, 0)),
                  pl.BlockSpec(memory_space=pl.ANY)],
        out_specs=pl.BlockSpec(memory_space=pl.ANY),
        scratch_shapes=[pltpu.SemaphoreType.DMA(())],
        input_output_aliases={2: 0},
        compiler_params=pltpu.CompilerParams(
            dimension_semantics=("arbitrary",), has_side_effects=True),
        name="dispatch",
    )(dest_flat, x2p, rows_init)


def _experts_kernel(be_ref, first_ref, nvalid_ref, rows_ref, wgu_ref, bgu_ref, wd_ref, bd_ref,
                    y_ref, wgu_bf, wd_bf):
    i = pl.program_id(0)

    @pl.when(i < nvalid_ref[0])
    def _():
        @pl.when(first_ref[i] == 1)
        def _():
            wgu_bf[...] = wgu_ref[...].astype(BF16)
            wd_bf[...] = wd_ref[...].astype(BF16)

        lo, hi = _unpack_bf16_pairs(rows_ref[...])
        half = D_MODEL // 2
        hgu = _dot(lo, wgu_bf[:half, :]) + _dot(hi, wgu_bf[half:, :]) + bgu_ref[...]
        g = jnp.minimum(hgu[:, :D_MODEL], SWIGLU_LIMIT)
        u = jnp.clip(hgu[:, D_MODEL:], -SWIGLU_LIMIT, SWIGLU_LIMIT)
        a = (u + 1.0) * (g * jax.nn.sigmoid(SWIGLU_ALPHA * g))
        y_ref[...] = _dot(a.astype(BF16), wd_bf[...]) + bd_ref[...]

    @pl.when(i >= nvalid_ref[0])
    def _():
        y_ref[...] = jnp.zeros_like(y_ref)


def _experts(layer, block_e, block_first, nvalid, rows, w_gu, b_gu, w_down, b_down):
    P, W = rows.shape
    L, E, D, F2 = w_gu.shape
    nb = P // BLK_E

    def row_map(i, be, bf, nv):
        return (jnp.minimum(i, nv[0] - 1), 0)

    def w_map(i, be, bf, nv):
        return (layer, be[i], 0, 0)

    grid_spec = pltpu.PrefetchScalarGridSpec(
        num_scalar_prefetch=3,
        grid=(nb,),
        in_specs=[pl.BlockSpec((BLK_E, W), row_map),
                  pl.BlockSpec((None, None, D, F2), w_map),
                  pl.BlockSpec((None, None, 1, F2), w_map),
                  pl.BlockSpec((None, None, F2 // 2, D), w_map),
                  pl.BlockSpec((None, None, 1, D), w_map)],
        out_specs=pl.BlockSpec((BLK_E, D), lambda i, be, bf, nv: (i, 0)),
        scratch_shapes=[pltpu.VMEM((D, F2), BF16), pltpu.VMEM((F2 // 2, D), BF16)],
    )
    return pl.pallas_call(
        _experts_kernel,
        out_shape=jax.ShapeDtypeStruct((P, D), F32),
        grid_spec=grid_spec,
        compiler_params=pltpu.CompilerParams(
            dimension_semantics=("arbitrary",), vmem_limit_bytes=VMEM_LIMIT),
        name="experts",
    )(block_e, block_first, nvalid, rows, w_gu, b_gu.reshape(L, E, 1, F2), w_down,
      b_down.reshape(L, E, 1, D))


def _combine_kernel(dest_ref, gate_ref, x2_ref, lng_ref, lnb_ref, y_ref, o_ref, ybuf, sem):
    def start(t, carry):
        for k in range(TOP_K):
            _row_copy(y_ref, dest_ref[t * TOP_K + k], ybuf.at[k], t, sem).start()
        return carry

    lax.fori_loop(0, TD_ROWS, start, 0)

    def wait(t, carry):
        for k in range(TOP_K):
            _row_copy(y_ref, 0, ybuf.at[k], 0, sem).wait()
        return carry

    lax.fori_loop(0, TD_ROWS, wait, 0)

    gates = gate_ref[...]
    ff = gates[:, 0:1] * ybuf[0]
    for k in range(1, TOP_K):
        ff = ff + gates[:, k:k + 1] * ybuf[k]
    alpha = (2 * 2) ** 0.25
    o_ref[...] = _layer_norm(alpha * x2_ref[...] + ff, lng_ref[2:3, :], lnb_ref[2:3, :])


def _combine(dest_flat, gates, x2, ln_g, ln_b, y):
    N, D = x2.shape
    return pl.pallas_call(
        _combine_kernel,
        out_shape=jax.ShapeDtypeStruct((N, D), F32),
        grid=(N // TD_ROWS,),
        in_specs=[pl.BlockSpec((TD_ROWS * TOP_K,), lambda i: (i,), memory_space=pltpu.SMEM),
                  pl.BlockSpec((TD_ROWS, TOP_K), lambda i: (i, 0)),
                  pl.BlockSpec((TD_ROWS, D), lambda i: (i, 0)),
                  pl.BlockSpec(ln_g.shape, lambda i: (0, 0)),
                  pl.BlockSpec(ln_b.shape, lambda i: (0, 0)),
                  pl.BlockSpec(memory_space=pl.ANY)],
        out_specs=pl.BlockSpec((TD_ROWS, D), lambda i: (i, 0)),
        scratch_shapes=[pltpu.VMEM((TOP_K, TD_ROWS, D), F32), pltpu.SemaphoreType.DMA(())],
        compiler_params=pltpu.CompilerParams(
            dimension_semantics=("arbitrary",), vmem_limit_bytes=VMEM_LIMIT),
        name="combine",
    )(dest_flat, gates, x2, ln_g, ln_b, y)


def _routing_plan(counts):
    counts = counts.reshape(N_EXPERTS).astype(jnp.int32)
    nblk = (counts + BLK_E - 1) // BLK_E
    bend = jnp.cumsum(nblk)
    pstart = (bend - nblk) * BLK_E
    nvalid = bend[-1:]
    return pstart, bend, nvalid


def _layer(layer, x, mem, w_in, sgu_g, sgu_b, w_sp, b_sp, grp_g, w_out, wq, wkv, wo,
           w_router, b_router, w_gu, b_gu, w_down, b_down, ln_g, ln_b):
    B, S, D = x.shape
    N = B * S
    row = lambda a: a.reshape(1, -1)

    ya, q, k, v = _mixer_in(x, w_in.astype(BF16), row(sgu_g), row(sgu_b), w_sp,
                            b_sp.reshape(SGU_GROUPS, CHUNK, 1), row(grp_g[:SGU_WIDTH]))
    yb = _sb_attn(q, k, v, row(grp_g[SGU_WIDTH:]))
    k_mem, v_mem = _kv_proj(mem, wkv.astype(BF16))
    x2, x2p, idx, rank, gates, counts = _post_mixer(
        x, ya, yb, w_out.astype(BF16), ln_g, ln_b, wq.astype(BF16), k_mem, v_mem,
        wo.astype(BF16), w_router, row(b_router))

    pstart, bend, nvalid = _routing_plan(counts)
    nb = (N * TOP_K) // BLK_E + N_EXPERTS
    dest = (pstart[idx.reshape(-1)] + rank.reshape(-1)).astype(jnp.int32)
    blocks = jnp.arange(nb, dtype=jnp.int32)
    block_e = jnp.minimum(jnp.searchsorted(bend, blocks, side="right"), N_EXPERTS - 1)
    block_e = jnp.where(blocks < nvalid[0], block_e, block_e[jnp.maximum(nvalid[0] - 1, 0)])
    block_e = block_e.astype(jnp.int32)
    block_first = jnp.concatenate(
        [jnp.ones((1,), jnp.int32), (block_e[1:] != block_e[:-1]).astype(jnp.int32)])

    rows = _dispatch(dest, x2p.reshape(N, D // 2), jnp.zeros((nb * BLK_E, D // 2), jnp.uint32))
    y = _experts(layer, block_e, block_first, nvalid, rows, w_gu, b_gu, w_down, b_down)
    out = _combine(dest, gates.reshape(N, TOP_K), x2.reshape(N, D), ln_g, ln_b, y)
    return out.reshape(B, S, D)


def kernel(x, mem, w_in, sgu_g, sgu_b, w_sp, b_sp, grp_g, w_out, wq_mem, wkv_mem, wo_mem,
           w_router, b_router, w_gu, b_gu, w_down, b_down, ln_g, ln_b):
    for l in range(w_in.shape[0]):
        x = _layer(l, x, mem, w_in[l], sgu_g[l], sgu_b[l], w_sp[l], b_sp[l], grp_g[l], w_out[l],
                   wq_mem[l], wkv_mem[l], wo_mem[l], w_router[l], b_router[l], w_gu, b_gu,
                   w_down, b_down, ln_g[l], ln_b[l])
    return x
```

```python
import jax
import jax.numpy as jnp
from jax import lax
from jax.experimental import pallas as pl
from jax.experimental.pallas import tpu as pltpu

F32 = jnp.float32
BF16 = jnp.bfloat16

D_MODEL = 1024
SGU_WIDTH = 512
SGU_GROUPS = 4
GROUP_DIM = 128
CHUNK = 128
SB_WIDTH = 512
SB_HEADS = 8
SB_HEAD_DIM = 64
MEM_HEADS = 4
MEM_HEAD_DIM = 256
N_EXPERTS = 32
TOP_K = 4
SWIGLU_LIMIT = 7.0
SWIGLU_ALPHA = 1.702
LN_EPS = 1e-5
RMS_EPS = 1e-6
LOG2E = 1.4426950408889634

LANES = 128
TM_MIX = 512
TQ_SB = 512
TM_POST = 512
TD_ROWS = 256
BLK_E = 256
VMEM_LIMIT = 56 * 1024 * 1024


def _layer_norm(x, g, b):
    mu = jnp.mean(x, axis=-1, keepdims=True)
    xc = x - mu
    var = jnp.mean(xc * xc, axis=-1, keepdims=True)
    return xc * lax.rsqrt(var + LN_EPS) * g + b


def _dot(a, b):
    return jnp.dot(a, b, preferred_element_type=F32)


def _dot_nt(a, b):
    return lax.dot_general(a, b, (((1,), (1,)), ((), ())), preferred_element_type=F32)


def _mixer_in_kernel(x_ref, win_ref, sg_ref, sb_ref, wsp_ref, bsp_ref, gg_ref,
                     ya_ref, q_ref, k_ref, v_ref):
    xb = x_ref[...].astype(BF16)

    def proj(lo, hi):
        return _dot(xb, win_ref[:, lo:hi])

    u = jax.nn.gelu(proj(0, SGU_WIDTH))
    vs = jax.nn.gelu(proj(SGU_WIDTH, 2 * SGU_WIDTH))
    r = lax.broadcasted_iota(jnp.int32, (CHUNK, CHUNK), 0)
    c = lax.broadcasted_iota(jnp.int32, (CHUNK, CHUNK), 1)
    causal = r >= c
    for g in range(SGU_GROUPS):
        gs = slice(g * GROUP_DIM, (g + 1) * GROUP_DIM)
        vn = _layer_norm(vs[:, gs], sg_ref[:, gs], sb_ref[:, gs]).astype(BF16)
        wc = jnp.where(causal, wsp_ref[g], 0.0).astype(BF16)
        for ch in range(TM_MIX // CHUNK):
            cs = slice(ch * CHUNK, (ch + 1) * CHUNK)
            gate = _dot(wc, vn[cs]) + bsp_ref[g]
            oa = u[cs, gs] * gate
            ms = jnp.mean(oa * oa, axis=-1, keepdims=True)
            ya_ref[cs, gs] = (oa * lax.rsqrt(ms + RMS_EPS) * gg_ref[:, gs]).astype(BF16)
    o = 2 * SGU_WIDTH
    q_ref[...] = (proj(o, o + SB_WIDTH) * (SB_HEAD_DIM ** -0.5)).astype(BF16)
    k_ref[...] = proj(o + SB_WIDTH, o + 2 * SB_WIDTH).astype(BF16)
    v_ref[...] = proj(o + 2 * SB_WIDTH, o + 3 * SB_WIDTH).astype(BF16)


def _mixer_in(x, w_in, sgu_g, sgu_b, w_sp, b_sp, grp_g_a):
    B, S, D = x.shape
    full = lambda shape: pl.BlockSpec(shape, lambda b, i: (0,) * len(shape))
    tile = lambda w: pl.BlockSpec((None, TM_MIX, w), lambda b, i: (b, i, 0))
    out = jax.ShapeDtypeStruct((B, S, SGU_WIDTH), BF16)
    return pl.pallas_call(
        _mixer_in_kernel,
        out_shape=(out, out, out, out),
        grid=(B, S // TM_MIX),
        in_specs=[tile(D), full(w_in.shape), full(sgu_g.shape), full(sgu_b.shape),
                  full(w_sp.shape), full(b_sp.shape), full(grp_g_a.shape)],
        out_specs=(tile(SGU_WIDTH),) * 4,
        compiler_params=pltpu.CompilerParams(
            dimension_semantics=("parallel", "parallel"), vmem_limit_bytes=VMEM_LIMIT),
        name="mixer_in",
    )(x, w_in, sgu_g, sgu_b, w_sp, b_sp, grp_g_a)


def _sb_attn_kernel(q_ref, k_ref, v_ref, gg_ref, o_ref):
    i = pl.program_id(2)
    lo = i * TQ_SB
    lane = lax.broadcasted_iota(jnp.int32, (1, LANES), 1)
    head0 = lane < SB_HEAD_DIM
    q_all = q_ref[...]
    r = lax.broadcasted_iota(jnp.int32, (2 * LANES, 2 * LANES), 0)
    c = lax.broadcasted_iota(jnp.int32, (2 * LANES, 2 * LANES), 1)
    after = jnp.where((r > c) & ((r // LANES) == (c // LANES)), 1.0, 0.0).astype(BF16)

    def split_heads(x):
        zx = jnp.zeros_like(x)
        return jnp.concatenate([jnp.where(head0, x, zx), jnp.where(head0, zx, x)], axis=0)

    def key_block(q, s0, carry, acc, t0):
        s0 = pl.multiple_of(s0, LANES)
        k2 = split_heads(k_ref[pl.ds(s0, LANES), :])
        v2 = split_heads(v_ref[pl.ds(s0, LANES), :])
        z = _dot_nt(q, k2)
        sp = jnp.maximum(z, 0.0) + jnp.log(1.0 + jnp.exp2(jnp.abs(z) * (-LOG2E)))
        if t0 is not None:
            t_pos = t0 + lax.broadcasted_iota(jnp.int32, z.shape, 0)
            s_pos = s0 + (lax.broadcasted_iota(jnp.int32, z.shape, 1) & (LANES - 1))
            mask = s_pos < t_pos
            sp = jnp.where(mask, sp, 0.0)
        hi = sp.astype(BF16)
        lo_part = (sp - hi.astype(F32)).astype(BF16)
        later = _dot(hi, after) + _dot(lo_part, after)
        a = jnp.exp(z - (sp + later + carry))
        if t0 is not None:
            a = jnp.where(mask, a, 0.0)
        acc = acc + _dot(a.astype(BF16), v2)
        m = z.shape[0]
        rows = [jnp.broadcast_to(jnp.sum(sp[:, h * LANES:(h + 1) * LANES], axis=1, keepdims=True),
                                 (m, LANES)) for h in range(2)]
        carry = carry + jnp.concatenate(rows, axis=1)
        return carry, acc

    nsub = TQ_SB // LANES
    carry = jnp.zeros((TQ_SB, 2 * LANES), F32)
    acc = jnp.zeros((TQ_SB, LANES), F32)
    for sb in reversed(range(nsub)):
        r0 = sb * LANES
        cr, ac = key_block(q_all[r0:], lo + r0, carry[r0:], acc[r0:], lo + r0)
        if r0:
            cr = jnp.concatenate([carry[:r0], cr], axis=0)
            ac = jnp.concatenate([acc[:r0], ac], axis=0)
        carry, acc = cr, ac

    def body(j, st):
        s0 = pl.multiple_of(lo - (j + 1) * TQ_SB, TQ_SB)
        cr, ac = st
        for sb in reversed(range(nsub)):
            cr, ac = key_block(q_all, s0 + sb * LANES, cr, ac, None)
        return cr, ac

    carry, acc = lax.fori_loop(0, i, body, (carry, acc))

    sq = acc * acc
    ss0 = jnp.sum(jnp.where(head0, sq, 0.0), axis=1, keepdims=True)
    ss1 = jnp.sum(jnp.where(head0, 0.0, sq), axis=1, keepdims=True)
    inv = 1.0 / SB_HEAD_DIM
    rs = jnp.where(head0, lax.rsqrt(ss0 * inv + RMS_EPS), lax.rsqrt(ss1 * inv + RMS_EPS))
    o_ref[...] = (acc * rs * gg_ref[...]).astype(BF16)


def _sb_attn(q, k, v, grp_g_b):
    B, S, W = q.shape
    pairs = W // LANES
    return pl.pallas_call(
        _sb_attn_kernel,
        out_shape=jax.ShapeDtypeStruct((B, S, W), BF16),
        grid=(B, pairs, S // TQ_SB),
        in_specs=[pl.BlockSpec((None, TQ_SB, LANES), lambda b, p, i: (b, i, p)),
                  pl.BlockSpec((None, S, LANES), lambda b, p, i: (b, 0, p)),
                  pl.BlockSpec((None, S, LANES), lambda b, p, i: (b, 0, p)),
                  pl.BlockSpec((1, LANES), lambda b, p, i: (0, p))],
        out_specs=pl.BlockSpec((None, TQ_SB, LANES), lambda b, p, i: (b, i, p)),
        compiler_params=pltpu.CompilerParams(
            dimension_semantics=("parallel", "parallel", "arbitrary"),
            vmem_limit_bytes=VMEM_LIMIT),
        name="sb_attn",
    )(q, k, v, grp_g_b)


def _kv_proj_kernel(mem_ref, wkv_ref, k_ref, v_ref):
    mb = mem_ref[...].astype(BF16)
    k_ref[...] = _dot(mb, wkv_ref[:, :D_MODEL]).astype(BF16)
    v_ref[...] = _dot(mb, wkv_ref[:, D_MODEL:]).astype(BF16)


def _kv_proj(mem, wkv):
    B, M, D = mem.shape
    out = jax.ShapeDtypeStruct((B, M, D), BF16)
    return pl.pallas_call(
        _kv_proj_kernel,
        out_shape=(out, out),
        grid=(B,),
        in_specs=[pl.BlockSpec((None, M, D), lambda b: (b, 0, 0)),
                  pl.BlockSpec(wkv.shape, lambda b: (0, 0))],
        out_specs=(pl.BlockSpec((None, M, D), lambda b: (b, 0, 0)),) * 2,
        compiler_params=pltpu.CompilerParams(
            dimension_semantics=("parallel",), vmem_limit_bytes=VMEM_LIMIT),
        name="kv_proj",
    )(mem, wkv)


def _pack_bf16_pairs(x):
    w = x.shape[1] // 2
    bits = lax.bitcast_convert_type(x.astype(BF16).astype(F32), jnp.uint32)
    return (bits[:, w:] & jnp.uint32(0xFFFF0000)) | (bits[:, :w] >> 16)


def _unpack_bf16_pairs(u):
    lo = lax.bitcast_convert_type(u << 16, F32).astype(BF16)
    hi = lax.bitcast_convert_type(u & jnp.uint32(0xFFFF0000), F32).astype(BF16)
    return lo, hi


def _post_mixer_kernel(x_ref, ya_ref, yb_ref, wout_ref, lng_ref, lnb_ref, wq_ref, km_ref, vm_ref,
                       wo_ref, wr_ref, br_ref,
                       x2_ref, x2p_ref, idx_ref, rank_ref, gate_ref, cnt_ref, carry_ref):
    first = jnp.logical_and(pl.program_id(0) == 0, pl.program_id(1) == 0)

    @pl.when(first)
    def _():
        carry_ref[...] = jnp.zeros_like(carry_ref)

    alpha = (2 * 2) ** 0.25
    mix = _dot(ya_ref[...], wout_ref[:SGU_WIDTH, :]) + _dot(yb_ref[...], wout_ref[SGU_WIDTH:, :])
    x1 = _layer_norm(alpha * x_ref[...] + mix, lng_ref[0:1, :], lnb_ref[0:1, :])

    qm = (_dot(x1.astype(BF16), wq_ref[...]) * (MEM_HEAD_DIM ** -0.5)).astype(BF16)
    heads = []
    for h in range(MEM_HEADS):
        hs = slice(h * MEM_HEAD_DIM, (h + 1) * MEM_HEAD_DIM)
        s = _dot_nt(qm[:, hs], km_ref[:, hs])
        e = jnp.exp(s - jnp.max(s, axis=-1, keepdims=True))
        p = (e / jnp.sum(e, axis=-1, keepdims=True)).astype(BF16)
        heads.append(_dot(p, vm_ref[:, hs]).astype(BF16))
    o = jnp.concatenate(heads, axis=1)
    x2 = _layer_norm(alpha * x1 + _dot(o, wo_ref[...]), lng_ref[1:2, :], lnb_ref[1:2, :])
    x2_ref[...] = x2
    x2p_ref[...] = _pack_bf16_pairs(x2)

    xh = x2.astype(BF16)
    xl = (x2 - xh.astype(F32)).astype(BF16)
    wr = wr_ref[...]
    wh = wr.astype(BF16)
    wl = (wr - wh.astype(F32)).astype(BF16)
    logits = _dot(xh, wh) + (_dot(xl, wh) + _dot(xh, wl)) + br_ref[...]

    tm = logits.shape[0]
    lane = lax.broadcasted_iota(jnp.int32, (tm, N_EXPERTS), 1).astype(F32)
    work = logits
    vals, idxs = [], []
    chosen = jnp.zeros((tm, N_EXPERTS), F32)
    for _ in range(TOP_K):
        m = jnp.max(work, axis=-1, keepdims=True)
        ix = jnp.min(jnp.where(work == m, lane, float(N_EXPERTS)), axis=-1, keepdims=True)
        sel = lane == ix
        work = jnp.where(sel, -jnp.inf, work)
        chosen = jnp.where(sel, 1.0, chosen)
        vals.append(m)
        idxs.append(ix)
    ev = [jnp.exp(v - vals[0]) for v in vals]
    den = ev[0] + ev[1] + ev[2] + ev[3]

    r = lax.broadcasted_iota(jnp.int32, (tm, tm), 0)
    c = lax.broadcasted_iota(jnp.int32, (tm, tm), 1)
    before = jnp.where(r > c, 1.0, 0.0).astype(BF16)
    running = _dot(before, chosen.astype(BF16)) + carry_ref[...]
    ranks = [jnp.sum(jnp.where(lane == ix, running, 0.0), axis=-1, keepdims=True) for ix in idxs]
    carry_ref[...] = carry_ref[...] + jnp.sum(chosen, axis=0, keepdims=True)
    cnt_ref[...] = carry_ref[...]

    l4 = lax.broadcasted_iota(jnp.int32, (tm, TOP_K), 1)

    def cols(xs):
        return jnp.where(l4 == 0, xs[0], jnp.where(l4 == 1, xs[1], jnp.where(l4 == 2, xs[2], xs[3])))

    idx_ref[...] = cols(idxs).astype(jnp.int32)
    rank_ref[...] = cols(ranks).astype(jnp.int32)
    gate_ref[...] = cols([e / den for e in ev])


def _post_mixer(x, ya, yb, w_out, ln_g, ln_b, wq, k_mem, v_mem, wo, w_router, b_router):
    B, S, D = x.shape
    M = k_mem.shape[1]
    full = lambda a: pl.BlockSpec(a.shape, lambda b, i: (0,) * a.ndim)
    tile = lambda w: pl.BlockSpec((None, TM_POST, w), lambda b, i: (b, i, 0))
    memb = pl.BlockSpec((None, M, D), lambda b, i: (b, 0, 0))
    sds = jax.ShapeDtypeStruct
    return pl.pallas_call(
        _post_mixer_kernel,
        out_shape=(sds((B, S, D), F32), sds((B, S, D // 2), jnp.uint32),
                   sds((B, S, TOP_K), jnp.int32), sds((B, S, TOP_K), jnp.int32),
                   sds((B, S, TOP_K), F32), sds((1, N_EXPERTS), F32)),
        grid=(B, S // TM_POST),
        in_specs=[tile(D), tile(SGU_WIDTH), tile(SB_WIDTH), full(w_out), full(ln_g), full(ln_b),
                  full(wq), memb, memb, full(wo), full(w_router), full(b_router)],
        out_specs=(tile(D), tile(D // 2), tile(TOP_K), tile(TOP_K), tile(TOP_K),
                   pl.BlockSpec((1, N_EXPERTS), lambda b, i: (0, 0))),
        scratch_shapes=[pltpu.VMEM((1, N_EXPERTS), F32)],
        compiler_params=pltpu.CompilerParams(
            dimension_semantics=("arbitrary", "arbitrary"), vmem_limit_bytes=VMEM_LIMIT),
        name="post_mixer",
    )(x, ya, yb, w_out, ln_g, ln_b, wq, k_mem, v_mem, wo, w_router, b_router)


def _row_copy(src_ref, src_row, dst_ref, dst_row, sem):
    return pltpu.make_async_copy(src_ref.at[pl.ds(src_row, 1)], dst_ref.at[pl.ds(dst_row, 1)], sem)


def _dest_row(pstart_ref, idx_ref, rank_ref, j):
    return pstart_ref[idx_ref[j]] + rank_ref[j]


def _dispatch_kernel(pstart_ref, bend_ref, idx_ref, rank_ref, x2p_ref, rows_ref, zbuf, sem, zsem):
    @pl.when(pl.program_id(0) == 0)
    def _():
        zbuf[...] = jnp.zeros_like(zbuf)

        def tail_fill(e):
            last = pl.multiple_of((bend_ref[e] - 1) * BLK_E, BLK_E)
            return pltpu.make_async_copy(zbuf, rows_ref.at[pl.ds(last, BLK_E)], zsem)

        def nonempty(e):
            return bend_ref[e] > (bend_ref[e - 1] if e else 0)

        for e in range(N_EXPERTS):
            pl.when(nonempty(e))(lambda e=e: tail_fill(e).start())
        for e in range(N_EXPERTS):
            pl.when(nonempty(e))(lambda e=e: tail_fill(e).wait())

        def unused_fill(b):
            return pltpu.make_async_copy(
                zbuf, rows_ref.at[pl.ds(pl.multiple_of(b * BLK_E, BLK_E), BLK_E)], zsem)

        used, total = bend_ref[N_EXPERTS - 1], rows_ref.shape[0] // BLK_E
        lax.fori_loop(used, total, lambda b, c: (unused_fill(b).start(), c)[1], 0)
        lax.fori_loop(used, total, lambda b, c: (unused_fill(b).wait(), c)[1], 0)

    def start(t, carry):
        for k in range(TOP_K):
            dest = _dest_row(pstart_ref, idx_ref, rank_ref, t * TOP_K + k)
            _row_copy(x2p_ref, t, rows_ref, dest, sem).start(priority=k % 2)
        return carry

    lax.fori_loop(0, TD_ROWS, start, 0)
    for k in range(TOP_K):
        pltpu.make_async_copy(x2p_ref, rows_ref.at[pl.ds(0, TD_ROWS)], sem).wait()


def _dispatch(pstart, bend, idx_flat, rank_flat, x2p, n_rows):
    N, W = x2p.shape
    smem_blk = pl.BlockSpec((TD_ROWS * TOP_K,), lambda i, ps, be: (i,), memory_space=pltpu.SMEM)
    grid_spec = pltpu.PrefetchScalarGridSpec(
        num_scalar_prefetch=2,
        grid=(N // TD_ROWS,),
        in_specs=[smem_blk, smem_blk, pl.BlockSpec((TD_ROWS, W), lambda i, ps, be: (i, 0))],
        out_specs=pl.BlockSpec(memory_space=pl.ANY),
        scratch_shapes=[pltpu.VMEM((BLK_E, W), jnp.uint32), pltpu.SemaphoreType.DMA(()),
                        pltpu.SemaphoreType.DMA(())],
    )
    return pl.pallas_call(
        _dispatch_kernel,
        out_shape=jax.ShapeDtypeStruct((n_rows, W), jnp.uint32),
        grid_spec=grid_spec,
        compiler_params=pltpu.CompilerParams(
            dimension_semantics=("arbitrary",), has_side_effects=True),
        name="dispatch",
    )(pstart, bend, idx_flat, rank_flat, x2p)


def _experts_kernel(be_ref, first_ref, nvalid_ref, rows_ref, wgu_ref, bgu_ref, wd_ref, bd_ref,
                    y_ref, wgu_bf, wd_bf):
    i = pl.program_id(0)

    @pl.when(i < nvalid_ref[0])
    def _():
        @pl.when(first_ref[i] == 1)
        def _():
            wgu_bf[...] = wgu_ref[...].astype(BF16)
            wd_bf[...] = wd_ref[...].astype(BF16)

        lo, hi = _unpack_bf16_pairs(rows_ref[...])
        half = D_MODEL // 2
        hgu = _dot(lo, wgu_bf[:half, :]) + _dot(hi, wgu_bf[half:, :]) + bgu_ref[...]
        g = jnp.minimum(hgu[:, :D_MODEL], SWIGLU_LIMIT)
        u = jnp.clip(hgu[:, D_MODEL:], -SWIGLU_LIMIT, SWIGLU_LIMIT)
        a = (u + 1.0) * (g * jax.nn.sigmoid(SWIGLU_ALPHA * g))
        y_ref[...] = _dot(a.astype(BF16), wd_bf[...]) + bd_ref[...]

    @pl.when(i >= nvalid_ref[0])
    def _():
        y_ref[...] = jnp.zeros_like(y_ref)


def _experts(layer, block_e, block_first, nvalid, rows, w_gu, b_gu, w_down, b_down):
    P, W = rows.shape
    L, E, D, F2 = w_gu.shape
    nb = P // BLK_E

    def row_map(i, be, bf, nv):
        return (jnp.minimum(i, nv[0] - 1), 0)

    def w_map(i, be, bf, nv):
        return (layer, be[i], 0, 0)

    grid_spec = pltpu.PrefetchScalarGridSpec(
        num_scalar_prefetch=3,
        grid=(nb,),
        in_specs=[pl.BlockSpec((BLK_E, W), row_map),
                  pl.BlockSpec((None, None, D, F2), w_map),
                  pl.BlockSpec((None, None, 1, F2), w_map),
                  pl.BlockSpec((None, None, F2 // 2, D), w_map),
                  pl.BlockSpec((None, None, 1, D), w_map)],
        out_specs=pl.BlockSpec((BLK_E, D), lambda i, be, bf, nv: (i, 0)),
        scratch_shapes=[pltpu.VMEM((D, F2), BF16), pltpu.VMEM((F2 // 2, D), BF16)],
    )
    return pl.pallas_call(
        _experts_kernel,
        out_shape=jax.ShapeDtypeStruct((P, D), F32),
        grid_spec=grid_spec,
        compiler_params=pltpu.CompilerParams(
            dimension_semantics=("arbitrary",), vmem_limit_bytes=VMEM_LIMIT),
        name="experts",
    )(block_e, block_first, nvalid, rows, w_gu, b_gu.reshape(L, E, 1, F2), w_down,
      b_down.reshape(L, E, 1, D))


def _combine_kernel(pstart_ref, idx_ref, rank_ref, gate_ref, x2_ref, lng_ref, lnb_ref, y_ref, o_ref,
                    ybuf, sem):
    def start(t, carry):
        for k in range(TOP_K):
            src = _dest_row(pstart_ref, idx_ref, rank_ref, t * TOP_K + k)
            _row_copy(y_ref, src, ybuf.at[k], t, sem).start(priority=k % 2)
        return carry

    lax.fori_loop(0, TD_ROWS, start, 0)
    for k in range(TOP_K):
        pltpu.make_async_copy(y_ref.at[pl.ds(0, TD_ROWS)], ybuf.at[k], sem).wait()

    gates = gate_ref[...]
    ff = gates[:, 0:1] * ybuf[0]
    for k in range(1, TOP_K):
        ff = ff + gates[:, k:k + 1] * ybuf[k]
    alpha = (2 * 2) ** 0.25
    o_ref[...] = _layer_norm(alpha * x2_ref[...] + ff, lng_ref[2:3, :], lnb_ref[2:3, :])


def _combine(pstart, idx_flat, rank_flat, gates, x2, ln_g, ln_b, y):
    N, D = x2.shape
    smem_blk = pl.BlockSpec((TD_ROWS * TOP_K,), lambda i, ps: (i,), memory_space=pltpu.SMEM)
    grid_spec = pltpu.PrefetchScalarGridSpec(
        num_scalar_prefetch=1,
        grid=(N // TD_ROWS,),
        in_specs=[smem_blk, smem_blk,
                  pl.BlockSpec((TD_ROWS, TOP_K), lambda i, ps: (i, 0)),
                  pl.BlockSpec((TD_ROWS, D), lambda i, ps: (i, 0)),
                  pl.BlockSpec(ln_g.shape, lambda i, ps: (0, 0)),
                  pl.BlockSpec(ln_b.shape, lambda i, ps: (0, 0)),
                  pl.BlockSpec(memory_space=pl.ANY)],
        out_specs=pl.BlockSpec((TD_ROWS, D), lambda i, ps: (i, 0)),
        scratch_shapes=[pltpu.VMEM((TOP_K, TD_ROWS, D), F32), pltpu.SemaphoreType.DMA(())],
    )
    return pl.pallas_call(
        _combine_kernel,
        out_shape=jax.ShapeDtypeStruct((N, D), F32),
        grid_spec=grid_spec,
        compiler_params=pltpu.CompilerParams(
            dimension_semantics=("arbitrary",), vmem_limit_bytes=VMEM_LIMIT),
        name="combine",
    )(pstart, idx_flat, rank_flat, gates, x2, ln_g, ln_b, y)


def _routing_plan(counts):
    counts = counts.reshape(N_EXPERTS).astype(jnp.int32)
    nblk = (counts + BLK_E - 1) // BLK_E
    bend = jnp.cumsum(nblk)
    pstart = (bend - nblk) * BLK_E
    nvalid = bend[-1:]
    return pstart, bend, nvalid


def _layer(layer, x, mem, w_in, sgu_g, sgu_b, w_sp, b_sp, grp_g, w_out, wq, wkv, wo,
           w_router, b_router, w_gu, b_gu, w_down, b_down, ln_g, ln_b):
    B, S, D = x.shape
    N = B * S
    row = lambda a: a.reshape(1, -1)

    ya, q, k, v = _mixer_in(x, w_in.astype(BF16), row(sgu_g), row(sgu_b), w_sp,
                            b_sp.reshape(SGU_GROUPS, CHUNK, 1), row(grp_g[:SGU_WIDTH]))
    yb = _sb_attn(q, k, v, row(grp_g[SGU_WIDTH:]))
    k_mem, v_mem = _kv_proj(mem, wkv.astype(BF16))
    x2, x2p, idx, rank, gates, counts = _post_mixer(
        x, ya, yb, w_out.astype(BF16), ln_g, ln_b, wq.astype(BF16), k_mem, v_mem,
        wo.astype(BF16), w_router, row(b_router))

    pstart, bend, nvalid = _routing_plan(counts)
    nb = (N * TOP_K) // BLK_E + N_EXPERTS
    blocks = jnp.minimum(jnp.arange(nb, dtype=jnp.int32), nvalid[0] - 1)
    block_e = jnp.sum((blocks[:, None] >= bend[None, :]).astype(jnp.int32), axis=1)
    block_e = jnp.minimum(block_e, N_EXPERTS - 1)
    block_first = jnp.concatenate(
        [jnp.ones((1,), jnp.int32), (block_e[1:] != block_e[:-1]).astype(jnp.int32)])

    idx_flat, rank_flat = idx.reshape(-1), rank.reshape(-1)
    rows = _dispatch(pstart, bend, idx_flat, rank_flat, x2p.reshape(N, D // 2), nb * BLK_E)
    y = _experts(layer, block_e, block_first, nvalid, rows, w_gu, b_gu, w_down, b_down)
    out = _combine(pstart, idx_flat, rank_flat, gates.reshape(N, TOP_K), x2.reshape(N, D),
                   ln_g, ln_b, y)
    return out.reshape(B, S, D)


def kernel(x, mem, w_in, sgu_g, sgu_b, w_sp, b_sp, grp_g, w_out, wq_mem, wkv_mem, wo_mem,
           w_router, b_router, w_gu, b_gu, w_down, b_down, ln_g, ln_b):
    for l in range(w_in.shape[0]):
        x = _layer(l, x, mem, w_in[l], sgu_g[l], sgu_b[l], w_sp[l], b_sp[l], grp_g[l], w_out[l],
                   wq_mem[l], wkv_mem[l], wo_mem[l], w_router[l], b_router[l], w_gu, b_gu,
                   w_down, b_down, ln_g[l], ln_b[l])
    return x
```

```python
import functools
from typing import NamedTuple

import jax
import jax.numpy as jnp
from jax import lax
from jax.experimental import pallas as pl
from jax.experimental.pallas import tpu as pltpu

F32 = jnp.float32
BF16 = jnp.bfloat16

D_MODEL = 1024
SGU_WIDTH = 512
SGU_GROUPS = 4
GROUP_DIM = 128
CHUNK = 128
SB_WIDTH = 512
SB_HEADS = 8
SB_HEAD_DIM = 64
MEM_HEADS = 4
MEM_HEAD_DIM = 256
N_EXPERTS = 32
TOP_K = 4
SWIGLU_LIMIT = 7.0
SWIGLU_ALPHA = 1.702
LN_EPS = 1e-5
RMS_EPS = 1e-6
LOG2E = 1.4426950408889634

LANES = 128
TM_MIX = 512
TQ_SB = 512
TM_POST = 512
TD_ROWS = 256
BLK_E = 256
VMEM_LIMIT = 56 * 1024 * 1024


def _layer_norm(x, g, b):
    mu = jnp.mean(x, axis=-1, keepdims=True)
    xc = x - mu
    var = jnp.mean(xc * xc, axis=-1, keepdims=True)
    return xc * lax.rsqrt(var + LN_EPS) * g + b


def _dot(a, b):
    return jnp.dot(a, b, preferred_element_type=F32)


def _dot_nt(a, b):
    return lax.dot_general(a, b, (((1,), (1,)), ((), ())), preferred_element_type=F32)


def _mixer_in_kernel(x_ref, win_ref, sg_ref, sb_ref, wsp_ref, bsp_ref, gg_ref,
                     ya_ref, q_ref, k_ref, v_ref):
    xb = x_ref[...].astype(BF16)

    def proj(lo, hi):
        return _dot(xb, win_ref[:, lo:hi])

    u = jax.nn.gelu(proj(0, SGU_WIDTH))
    vs = jax.nn.gelu(proj(SGU_WIDTH, 2 * SGU_WIDTH))
    r = lax.broadcasted_iota(jnp.int32, (CHUNK, CHUNK), 0)
    c = lax.broadcasted_iota(jnp.int32, (CHUNK, CHUNK), 1)
    causal = r >= c
    for g in range(SGU_GROUPS):
        gs = slice(g * GROUP_DIM, (g + 1) * GROUP_DIM)
        vn = _layer_norm(vs[:, gs], sg_ref[:, gs], sb_ref[:, gs]).astype(BF16)
        wc = jnp.where(causal, wsp_ref[g], 0.0).astype(BF16)
        for ch in range(TM_MIX // CHUNK):
            cs = slice(ch * CHUNK, (ch + 1) * CHUNK)
            gate = _dot(wc, vn[cs]) + bsp_ref[g]
            oa = u[cs, gs] * gate
            ms = jnp.mean(oa * oa, axis=-1, keepdims=True)
            ya_ref[cs, gs] = (oa * lax.rsqrt(ms + RMS_EPS) * gg_ref[:, gs]).astype(BF16)
    o = 2 * SGU_WIDTH
    q_ref[...] = (proj(o, o + SB_WIDTH) * (SB_HEAD_DIM ** -0.5)).astype(BF16)
    k_ref[...] = proj(o + SB_WIDTH, o + 2 * SB_WIDTH).astype(BF16)
    v_ref[...] = proj(o + 2 * SB_WIDTH, o + 3 * SB_WIDTH).astype(BF16)


def _mixer_in(x, w_in, sgu_g, sgu_b, w_sp, b_sp, grp_g_a):
    B, S, D = x.shape
    full = lambda shape: pl.BlockSpec(shape, lambda b, i: (0,) * len(shape))
    tile = lambda w: pl.BlockSpec((None, TM_MIX, w), lambda b, i: (b, i, 0))
    out = jax.ShapeDtypeStruct((B, S, SGU_WIDTH), BF16)
    return pl.pallas_call(
        _mixer_in_kernel,
        out_shape=(out, out, out, out),
        grid=(B, S // TM_MIX),
        in_specs=[tile(D), full(w_in.shape), full(sgu_g.shape), full(sgu_b.shape),
                  full(w_sp.shape), full(b_sp.shape), full(grp_g_a.shape)],
        out_specs=(tile(SGU_WIDTH),) * 4,
        compiler_params=pltpu.CompilerParams(
            dimension_semantics=("parallel", "parallel"), vmem_limit_bytes=VMEM_LIMIT),
        name="mixer_in",
    )(x, w_in, sgu_g, sgu_b, w_sp, b_sp, grp_g_a)


def _sb_attn_kernel(q_ref, k_ref, v_ref, gg_ref, o_ref):
    i = pl.program_id(2)
    lo = i * TQ_SB
    lane = lax.broadcasted_iota(jnp.int32, (1, LANES), 1)
    head0 = lane < SB_HEAD_DIM
    q_all = q_ref[...]
    r = lax.broadcasted_iota(jnp.int32, (2 * LANES, 2 * LANES), 0)
    c = lax.broadcasted_iota(jnp.int32, (2 * LANES, 2 * LANES), 1)
    after = jnp.where((r > c) & ((r // LANES) == (c // LANES)), 1.0, 0.0).astype(BF16)

    def split_heads(x):
        zx = jnp.zeros_like(x)
        return jnp.concatenate([jnp.where(head0, x, zx), jnp.where(head0, zx, x)], axis=0)

    def add_rows(x, r0, upd):
        new = x[r0:] + upd
        return jnp.concatenate([x[:r0], new], axis=0) if r0 else new

    def key_blocks(blocks, carry, acc):
        starts = [pl.multiple_of(s0, LANES) for _, s0, _ in blocks]
        k2s = [split_heads(k_ref[pl.ds(s, LANES), :]) for s in starts]
        v2s = [split_heads(v_ref[pl.ds(s, LANES), :]) for s in starts]
        zs = [_dot_nt(q_all[r0:], k2) for (r0, _, _), k2 in zip(blocks, k2s)]
        sps, masks = [], []
        for (r0, _, causal), s0, z in zip(blocks, starts, zs):
            sp = jnp.maximum(z, 0.0) + jnp.log(1.0 + jnp.exp2(jnp.abs(z) * (-LOG2E)))
            mask = None
            if causal:
                t_pos = lo + r0 + lax.broadcasted_iota(jnp.int32, z.shape, 0)
                s_pos = s0 + (lax.broadcasted_iota(jnp.int32, z.shape, 1) & (LANES - 1))
                mask = s_pos < t_pos
                sp = jnp.where(mask, sp, 0.0)
            sps.append(sp)
            masks.append(mask)
        his = [sp.astype(BF16) for sp in sps]
        los = [(sp - hi.astype(F32)).astype(BF16) for sp, hi in zip(sps, his)]
        laters = [_dot(hi, after) + _dot(lo_part, after) for hi, lo_part in zip(his, los)]
        weights = []
        for (r0, _, _), z, sp, later, mask in zip(blocks, zs, sps, laters, masks):
            a = jnp.exp(z - (sp + later + carry[r0:]))
            if mask is not None:
                a = jnp.where(mask, a, 0.0)
            weights.append(a.astype(BF16))
            rows = [jnp.broadcast_to(jnp.sum(sp[:, h * LANES:(h + 1) * LANES], axis=1, keepdims=True),
                                     (z.shape[0], LANES)) for h in range(2)]
            carry = add_rows(carry, r0, jnp.concatenate(rows, axis=1))
        for (r0, _, _), a, v2 in zip(blocks, weights, v2s):
            acc = add_rows(acc, r0, _dot(a, v2))
        return carry, acc

    nsub = TQ_SB // LANES
    carry = jnp.zeros((TQ_SB, 2 * LANES), F32)
    acc = jnp.zeros((TQ_SB, LANES), F32)
    carry, acc = key_blocks([(sb * LANES, lo + sb * LANES, True) for sb in reversed(range(nsub))],
                            carry, acc)

    def body(j, st):
        s0 = pl.multiple_of(lo - (j + 1) * TQ_SB, TQ_SB)
        return key_blocks([(0, s0 + sb * LANES, False) for sb in reversed(range(nsub))], *st)

    carry, acc = lax.fori_loop(0, i, body, (carry, acc))

    sq = acc * acc
    ss0 = jnp.sum(jnp.where(head0, sq, 0.0), axis=1, keepdims=True)
    ss1 = jnp.sum(jnp.where(head0, 0.0, sq), axis=1, keepdims=True)
    inv = 1.0 / SB_HEAD_DIM
    rs = jnp.where(head0, lax.rsqrt(ss0 * inv + RMS_EPS), lax.rsqrt(ss1 * inv + RMS_EPS))
    o_ref[...] = (acc * rs * gg_ref[...]).astype(BF16)


def _sb_attn(q, k, v, grp_g_b):
    B, S, W = q.shape
    pairs = W // LANES
    return pl.pallas_call(
        _sb_attn_kernel,
        out_shape=jax.ShapeDtypeStruct((B, S, W), BF16),
        grid=(B, pairs, S // TQ_SB),
        in_specs=[pl.BlockSpec((None, TQ_SB, LANES), lambda b, p, i: (b, i, p)),
                  pl.BlockSpec((None, S, LANES), lambda b, p, i: (b, 0, p)),
                  pl.BlockSpec((None, S, LANES), lambda b, p, i: (b, 0, p)),
                  pl.BlockSpec((1, LANES), lambda b, p, i: (0, p))],
        out_specs=pl.BlockSpec((None, TQ_SB, LANES), lambda b, p, i: (b, i, p)),
        compiler_params=pltpu.CompilerParams(
            dimension_semantics=("parallel", "parallel", "arbitrary"),
            vmem_limit_bytes=VMEM_LIMIT),
        name="sb_attn",
    )(q, k, v, grp_g_b)


def _kv_proj_kernel(mem_ref, wkv_ref, k_ref, v_ref):
    mb = mem_ref[...].astype(BF16)
    k_ref[...] = _dot(mb, wkv_ref[:, :D_MODEL]).astype(BF16)
    v_ref[...] = _dot(mb, wkv_ref[:, D_MODEL:]).astype(BF16)


def _kv_proj(mem, wkv):
    B, M, D = mem.shape
    out = jax.ShapeDtypeStruct((B, M, D), BF16)
    return pl.pallas_call(
        _kv_proj_kernel,
        out_shape=(out, out),
        grid=(B,),
        in_specs=[pl.BlockSpec((None, M, D), lambda b: (b, 0, 0)),
                  pl.BlockSpec(wkv.shape, lambda b: (0, 0))],
        out_specs=(pl.BlockSpec((None, M, D), lambda b: (b, 0, 0)),) * 2,
        compiler_params=pltpu.CompilerParams(
            dimension_semantics=("parallel",), vmem_limit_bytes=VMEM_LIMIT),
        name="kv_proj",
    )(mem, wkv)


def _pack_bf16_pairs(x):
    w = x.shape[1] // 2
    bits = lax.bitcast_convert_type(x.astype(BF16).astype(F32), jnp.uint32)
    return (bits[:, w:] & jnp.uint32(0xFFFF0000)) | (bits[:, :w] >> 16)


def _unpack_bf16_pairs(u):
    lo = lax.bitcast_convert_type(u << 16, F32).astype(BF16)
    hi = lax.bitcast_convert_type(u & jnp.uint32(0xFFFF0000), F32).astype(BF16)
    return lo, hi


def _post_mixer_kernel(x_ref, ya_ref, yb_ref, wout_ref, lng_ref, lnb_ref, wq_ref, km_ref, vm_ref,
                       wo_ref, wr_ref, br_ref,
                       x2_ref, x2p_ref, idx_ref, rank_ref, gate_ref, cnt_ref, carry_ref):
    first = jnp.logical_and(pl.program_id(0) == 0, pl.program_id(1) == 0)

    @pl.when(first)
    def _():
        carry_ref[...] = jnp.zeros_like(carry_ref)

    alpha = (2 * 2) ** 0.25
    mix = _dot(ya_ref[...], wout_ref[:SGU_WIDTH, :]) + _dot(yb_ref[...], wout_ref[SGU_WIDTH:, :])
    x1 = _layer_norm(alpha * x_ref[...] + mix, lng_ref[0:1, :], lnb_ref[0:1, :])

    qm = (_dot(x1.astype(BF16), wq_ref[...]) * (MEM_HEAD_DIM ** -0.5)).astype(BF16)
    heads = []
    for h in range(MEM_HEADS):
        hs = slice(h * MEM_HEAD_DIM, (h + 1) * MEM_HEAD_DIM)
        s = _dot_nt(qm[:, hs], km_ref[:, hs])
        e = jnp.exp(s - jnp.max(s, axis=-1, keepdims=True))
        p = (e / jnp.sum(e, axis=-1, keepdims=True)).astype(BF16)
        heads.append(_dot(p, vm_ref[:, hs]).astype(BF16))
    o = jnp.concatenate(heads, axis=1)
    x2 = _layer_norm(alpha * x1 + _dot(o, wo_ref[...]), lng_ref[1:2, :], lnb_ref[1:2, :])
    x2_ref[...] = x2
    x2p_ref[...] = _pack_bf16_pairs(x2)

    xh = x2.astype(BF16)
    xl = (x2 - xh.astype(F32)).astype(BF16)
    wr = wr_ref[...]
    wh = wr.astype(BF16)
    wl = (wr - wh.astype(F32)).astype(BF16)
    logits = _dot(xh, wh) + (_dot(xl, wh) + _dot(xh, wl)) + br_ref[...]

    tm = logits.shape[0]
    lane = lax.broadcasted_iota(jnp.int32, (tm, N_EXPERTS), 1).astype(F32)
    work = logits
    vals, idxs = [], []
    chosen = jnp.zeros((tm, N_EXPERTS), F32)
    for _ in range(TOP_K):
        m = jnp.max(work, axis=-1, keepdims=True)
        ix = jnp.min(jnp.where(work == m, lane, float(N_EXPERTS)), axis=-1, keepdims=True)
        sel = lane == ix
        work = jnp.where(sel, -jnp.inf, work)
        chosen = jnp.where(sel, 1.0, chosen)
        vals.append(m)
        idxs.append(ix)
    ev = [jnp.exp(v - vals[0]) for v in vals]
    den = ev[0] + ev[1] + ev[2] + ev[3]

    r = lax.broadcasted_iota(jnp.int32, (tm, tm), 0)
    c = lax.broadcasted_iota(jnp.int32, (tm, tm), 1)
    before = jnp.where(r > c, 1.0, 0.0).astype(BF16)
    running = _dot(before, chosen.astype(BF16)) + carry_ref[...]
    ranks = [jnp.sum(jnp.where(lane == ix, running, 0.0), axis=-1, keepdims=True) for ix in idxs]
    carry_ref[...] = carry_ref[...] + jnp.sum(chosen, axis=0, keepdims=True)
    cnt_ref[...] = carry_ref[...]

    l4 = lax.broadcasted_iota(jnp.int32, (tm, TOP_K), 1)

    def cols(xs):
        return jnp.where(l4 == 0, xs[0], jnp.where(l4 == 1, xs[1], jnp.where(l4 == 2, xs[2], xs[3])))

    idx_ref[...] = cols(idxs).astype(jnp.int32)
    rank_ref[...] = cols(ranks).astype(jnp.int32)
    gate_ref[...] = cols([e / den for e in ev])


def _post_mixer(x, ya, yb, w_out, ln_g, ln_b, wq, k_mem, v_mem, wo, w_router, b_router):
    B, S, D = x.shape
    M = k_mem.shape[1]
    full = lambda a: pl.BlockSpec(a.shape, lambda b, i: (0,) * a.ndim)
    tile = lambda w: pl.BlockSpec((None, TM_POST, w), lambda b, i: (b, i, 0))
    memb = pl.BlockSpec((None, M, D), lambda b, i: (b, 0, 0))
    sds = jax.ShapeDtypeStruct
    return pl.pallas_call(
        _post_mixer_kernel,
        out_shape=(sds((B, S, D), F32), sds((B, S, D // 2), jnp.uint32),
                   sds((B, S, TOP_K), jnp.int32), sds((B, S, TOP_K), jnp.int32),
                   sds((B, S, TOP_K), F32), sds((1, N_EXPERTS), F32)),
        grid=(B, S // TM_POST),
        in_specs=[tile(D), tile(SGU_WIDTH), tile(SB_WIDTH), full(w_out), full(ln_g), full(ln_b),
                  full(wq), memb, memb, full(wo), full(w_router), full(b_router)],
        out_specs=(tile(D), tile(D // 2), tile(TOP_K), tile(TOP_K), tile(TOP_K),
                   pl.BlockSpec((1, N_EXPERTS), lambda b, i: (0, 0))),
        scratch_shapes=[pltpu.VMEM((1, N_EXPERTS), F32)],
        compiler_params=pltpu.CompilerParams(
            dimension_semantics=("arbitrary", "arbitrary"), vmem_limit_bytes=VMEM_LIMIT),
        name="post_mixer",
    )(x, ya, yb, w_out, ln_g, ln_b, wq, k_mem, v_mem, wo, w_router, b_router)


def _row_copy(src_ref, src_row, dst_ref, dst_row, sem):
    return pltpu.make_async_copy(src_ref.at[pl.ds(src_row, 1)], dst_ref.at[pl.ds(dst_row, 1)], sem)


def _dest_row(pstart_ref, idx_ref, rank_ref, j):
    return pstart_ref[idx_ref[j]] + rank_ref[j]


def _dispatch_kernel(pstart_ref, bend_ref, idx_ref, rank_ref, x2p_ref, rows_ref, zbuf, sem, zsem):
    @pl.when(pl.program_id(0) == 0)
    def _():
        zbuf[...] = jnp.zeros_like(zbuf)

        def tail_fill(e):
            last = pl.multiple_of((bend_ref[e] - 1) * BLK_E, BLK_E)
            return pltpu.make_async_copy(zbuf, rows_ref.at[pl.ds(last, BLK_E)], zsem)

        def nonempty(e):
            return bend_ref[e] > (bend_ref[e - 1] if e else 0)

        for e in range(N_EXPERTS):
            pl.when(nonempty(e))(lambda e=e: tail_fill(e).start())
        for e in range(N_EXPERTS):
            pl.when(nonempty(e))(lambda e=e: tail_fill(e).wait())

        def unused_fill(b):
            return pltpu.make_async_copy(
                zbuf, rows_ref.at[pl.ds(pl.multiple_of(b * BLK_E, BLK_E), BLK_E)], zsem)

        used, total = bend_ref[N_EXPERTS - 1], rows_ref.shape[0] // BLK_E
        lax.fori_loop(used, total, lambda b, c: (unused_fill(b).start(), c)[1], 0)
        lax.fori_loop(used, total, lambda b, c: (unused_fill(b).wait(), c)[1], 0)

    def start(t, carry):
        for k in range(TOP_K):
            dest = _dest_row(pstart_ref, idx_ref, rank_ref, t * TOP_K + k)
            _row_copy(x2p_ref, t, rows_ref, dest, sem).start(priority=k % 2)
        return carry

    lax.fori_loop(0, TD_ROWS, start, 0)
    for k in range(TOP_K):
        pltpu.make_async_copy(x2p_ref, rows_ref.at[pl.ds(0, TD_ROWS)], sem).wait()


def _dispatch(pstart, bend, idx_flat, rank_flat, x2p, n_rows):
    N, W = x2p.shape
    smem_blk = pl.BlockSpec((TD_ROWS * TOP_K,), lambda i, ps, be: (i,), memory_space=pltpu.SMEM)
    grid_spec = pltpu.PrefetchScalarGridSpec(
        num_scalar_prefetch=2,
        grid=(N // TD_ROWS,),
        in_specs=[smem_blk, smem_blk, pl.BlockSpec((TD_ROWS, W), lambda i, ps, be: (i, 0))],
        out_specs=pl.BlockSpec(memory_space=pl.ANY),
        scratch_shapes=[pltpu.VMEM((BLK_E, W), jnp.uint32), pltpu.SemaphoreType.DMA(()),
                        pltpu.SemaphoreType.DMA(())],
    )
    return pl.pallas_call(
        _dispatch_kernel,
        out_shape=jax.ShapeDtypeStruct((n_rows, W), jnp.uint32),
        grid_spec=grid_spec,
        compiler_params=pltpu.CompilerParams(
            dimension_semantics=("arbitrary",), has_side_effects=True),
        name="dispatch",
    )(pstart, bend, idx_flat, rank_flat, x2p)


def _experts_kernel(layer, be_ref, first_ref, nvalid_ref, next_ref, slot_ref,
                    rows_ref, bgu_ref, bd_ref, wgu_hbm, wd_hbm,
                    y_ref, wgu_f32, wd_f32, wgu_bf, wd_bf, sem):
    i = pl.program_id(0)

    def fetch(e, slot):
        return (pltpu.make_async_copy(wgu_hbm.at[layer, e], wgu_f32.at[slot], sem.at[0, slot]),
                pltpu.make_async_copy(wd_hbm.at[layer, e], wd_f32.at[slot], sem.at[1, slot]))

    @pl.when(i < nvalid_ref[0])
    def _():
        @pl.when(first_ref[i] == 1)
        def _():
            slot = slot_ref[i]

            @pl.when(i == 0)
            def _():
                for c in fetch(be_ref[i], slot):
                    c.start()

            for c in fetch(be_ref[i], slot):
                c.wait()
            wgu_bf[...] = wgu_f32[slot].astype(BF16)
            wd_bf[...] = wd_f32[slot].astype(BF16)

            @pl.when(next_ref[i] >= 0)
            def _():
                for c in fetch(next_ref[i], 1 - slot):
                    c.start()

        lo, hi = _unpack_bf16_pairs(rows_ref[...])
        half = D_MODEL // 2
        hgu = _dot(lo, wgu_bf[:half, :]) + _dot(hi, wgu_bf[half:, :]) + bgu_ref[...]
        g = jnp.minimum(hgu[:, :D_MODEL], SWIGLU_LIMIT)
        u = jnp.clip(hgu[:, D_MODEL:], -SWIGLU_LIMIT, SWIGLU_LIMIT)
        a = (u + 1.0) * (g * jax.nn.sigmoid(SWIGLU_ALPHA * g))
        y_ref[...] = _dot(a.astype(BF16), wd_bf[...]) + bd_ref[...]

    @pl.when(i >= nvalid_ref[0])
    def _():
        y_ref[...] = jnp.zeros_like(y_ref)


def _experts(layer, plan, rows, w_gu, b_gu, w_down, b_down):
    P, W = rows.shape
    L, E, D, F2 = w_gu.shape
    nb = P // BLK_E

    def row_map(i, be, bf, nv, nx, sl):
        return (jnp.minimum(i, nv[0] - 1), 0)

    def b_map(i, be, bf, nv, nx, sl):
        return (layer, be[i], 0, 0)

    grid_spec = pltpu.PrefetchScalarGridSpec(
        num_scalar_prefetch=5,
        grid=(nb,),
        in_specs=[pl.BlockSpec((BLK_E, W), row_map),
                  pl.BlockSpec((None, None, 1, F2), b_map),
                  pl.BlockSpec((None, None, 1, D), b_map),
                  pl.BlockSpec(memory_space=pl.ANY),
                  pl.BlockSpec(memory_space=pl.ANY)],
        out_specs=pl.BlockSpec((BLK_E, D), lambda i, *_: (i, 0)),
        scratch_shapes=[pltpu.VMEM((2, D, F2), F32), pltpu.VMEM((2, F2 // 2, D), F32),
                        pltpu.VMEM((D, F2), BF16), pltpu.VMEM((F2 // 2, D), BF16),
                        pltpu.SemaphoreType.DMA((2, 2))],
    )
    return pl.pallas_call(
        functools.partial(_experts_kernel, layer),
        out_shape=jax.ShapeDtypeStruct((P, D), F32),
        grid_spec=grid_spec,
        compiler_params=pltpu.CompilerParams(
            dimension_semantics=("arbitrary",), vmem_limit_bytes=VMEM_LIMIT),
        name="experts",
    )(plan.block_e, plan.block_first, plan.nvalid, plan.next_e, plan.slot, rows,
      b_gu.reshape(L, E, 1, F2), b_down.reshape(L, E, 1, D), w_gu, w_down)


def _combine_kernel(pstart_ref, idx_ref, rank_ref, gate_ref, x2_ref, lng_ref, lnb_ref, y_ref, o_ref,
                    ybuf, sem):
    def start(t, carry):
        for k in range(TOP_K):
            src = _dest_row(pstart_ref, idx_ref, rank_ref, t * TOP_K + k)
            _row_copy(y_ref, src, ybuf.at[k], t, sem).start(priority=k % 2)
        return carry

    lax.fori_loop(0, TD_ROWS, start, 0)
    for k in range(TOP_K):
        pltpu.make_async_copy(y_ref.at[pl.ds(0, TD_ROWS)], ybuf.at[k], sem).wait()

    gates = gate_ref[...]
    ff = gates[:, 0:1] * ybuf[0]
    for k in range(1, TOP_K):
        ff = ff + gates[:, k:k + 1] * ybuf[k]
    alpha = (2 * 2) ** 0.25
    o_ref[...] = _layer_norm(alpha * x2_ref[...] + ff, lng_ref[2:3, :], lnb_ref[2:3, :])


def _combine(pstart, idx_flat, rank_flat, gates, x2, ln_g, ln_b, y):
    N, D = x2.shape
    smem_blk = pl.BlockSpec((TD_ROWS * TOP_K,), lambda i, ps: (i,), memory_space=pltpu.SMEM)
    grid_spec = pltpu.PrefetchScalarGridSpec(
        num_scalar_prefetch=1,
        grid=(N // TD_ROWS,),
        in_specs=[smem_blk, smem_blk,
                  pl.BlockSpec((TD_ROWS, TOP_K), lambda i, ps: (i, 0)),
                  pl.BlockSpec((TD_ROWS, D), lambda i, ps: (i, 0)),
                  pl.BlockSpec(ln_g.shape, lambda i, ps: (0, 0)),
                  pl.BlockSpec(ln_b.shape, lambda i, ps: (0, 0)),
                  pl.BlockSpec(memory_space=pl.ANY)],
        out_specs=pl.BlockSpec((TD_ROWS, D), lambda i, ps: (i, 0)),
        scratch_shapes=[pltpu.VMEM((TOP_K, TD_ROWS, D), F32), pltpu.SemaphoreType.DMA(())],
    )
    return pl.pallas_call(
        _combine_kernel,
        out_shape=jax.ShapeDtypeStruct((N, D), F32),
        grid_spec=grid_spec,
        compiler_params=pltpu.CompilerParams(
            dimension_semantics=("arbitrary",), vmem_limit_bytes=VMEM_LIMIT),
        name="combine",
    )(pstart, idx_flat, rank_flat, gates, x2, ln_g, ln_b, y)


class _Plan(NamedTuple):
    pstart: jax.Array
    bend: jax.Array
    nvalid: jax.Array
    block_e: jax.Array
    block_first: jax.Array
    next_e: jax.Array
    slot: jax.Array


def _routing_plan(counts, nb):
    counts = counts.reshape(N_EXPERTS).astype(jnp.int32)
    nblk = (counts + BLK_E - 1) // BLK_E
    bend = jnp.cumsum(nblk)
    pstart = (bend - nblk) * BLK_E
    nvalid = bend[-1:]
    blocks = jnp.minimum(jnp.arange(nb, dtype=jnp.int32), nvalid[0] - 1)
    block_e = jnp.sum((blocks[:, None] >= bend[None, :]).astype(jnp.int32), axis=1)
    block_e = jnp.minimum(block_e, N_EXPERTS - 1)
    block_first = jnp.concatenate(
        [jnp.ones((1,), jnp.int32), (block_e[1:] != block_e[:-1]).astype(jnp.int32)])
    experts = jnp.arange(N_EXPERTS, dtype=jnp.int32)
    at_or_after = lax.cummin(jnp.where(nblk > 0, experts, N_EXPERTS), reverse=True)
    after = jnp.concatenate([at_or_after[1:], jnp.full((1,), N_EXPERTS, jnp.int32)])
    next_e = jnp.where(after < N_EXPERTS, after, -1)[block_e]
    slot = (jnp.cumsum(block_first) - 1) & 1
    return _Plan(pstart, bend, nvalid, block_e, block_first, next_e.astype(jnp.int32),
                 slot.astype(jnp.int32))


def _layer(layer, x, mem, w_in, sgu_g, sgu_b, w_sp, b_sp, grp_g, w_out, wq, wkv, wo,
           w_router, b_router, w_gu, b_gu, w_down, b_down, ln_g, ln_b):
    B, S, D = x.shape
    N = B * S
    row = lambda a: a.reshape(1, -1)

    ya, q, k, v = _mixer_in(x, w_in.astype(BF16), row(sgu_g), row(sgu_b), w_sp,
                            b_sp.reshape(SGU_GROUPS, CHUNK, 1), row(grp_g[:SGU_WIDTH]))
    yb = _sb_attn(q, k, v, row(grp_g[SGU_WIDTH:]))
    k_mem, v_mem = _kv_proj(mem, wkv.astype(BF16))
    x2, x2p, idx, rank, gates, counts = _post_mixer(
        x, ya, yb, w_out.astype(BF16), ln_g, ln_b, wq.astype(BF16), k_mem, v_mem,
        wo.astype(BF16), w_router, row(b_router))

    nb = (N * TOP_K) // BLK_E + N_EXPERTS
    plan = _routing_plan(counts, nb)
    idx_flat, rank_flat = idx.reshape(-1), rank.reshape(-1)
    rows = _dispatch(plan.pstart, plan.bend, idx_flat, rank_flat, x2p.reshape(N, D // 2),
                     nb * BLK_E)
    y = _experts(layer, plan, rows, w_gu, b_gu, w_down, b_down)
    out = _combine(plan.pstart, idx_flat, rank_flat, gates.reshape(N, TOP_K), x2.reshape(N, D),
                   ln_g, ln_b, y)
    return out.reshape(B, S, D)


def kernel(x, mem, w_in, sgu_g, sgu_b, w_sp, b_sp, grp_g, w_out, wq_mem, wkv_mem, wo_mem,
           w_router, b_router, w_gu, b_gu, w_down, b_down, ln_g, ln_b):
    for l in range(w_in.shape[0]):
        x = _layer(l, x, mem, w_in[l], sgu_g[l], sgu_b[l], w_sp[l], b_sp[l], grp_g[l], w_out[l],
                   wq_mem[l], wkv_mem[l], wo_mem[l], w_router[l], b_router[l], w_gu, b_gu,
                   w_down, b_down, ln_g[l], ln_b[l])
    return x
```

```python
import functools
from typing import NamedTuple

import jax
import jax.numpy as jnp
from jax import lax
from jax.experimental import pallas as pl
from jax.experimental.pallas import tpu as pltpu

F32 = jnp.float32
BF16 = jnp.bfloat16

D_MODEL = 1024
SGU_WIDTH = 512
SGU_GROUPS = 4
GROUP_DIM = 128
CHUNK = 128
SB_WIDTH = 512
SB_HEADS = 8
SB_HEAD_DIM = 64
MEM_HEADS = 4
MEM_HEAD_DIM = 256
N_EXPERTS = 32
TOP_K = 4
SWIGLU_LIMIT = 7.0
SWIGLU_ALPHA = 1.702
LN_EPS = 1e-5
RMS_EPS = 1e-6
LOG2E = 1.4426950408889634

LANES = 128
TM_MIX = 512
TQ_SB = 512
TM_POST = 512
TD_ROWS = 256
BLK_E = 256
DMA_UNROLL = 4
VMEM_LIMIT = 56 * 1024 * 1024


def _layer_norm(x, g, b):
    mu = jnp.mean(x, axis=-1, keepdims=True)
    xc = x - mu
    var = jnp.mean(xc * xc, axis=-1, keepdims=True)
    return xc * lax.rsqrt(var + LN_EPS) * g + b


def _dot(a, b):
    return jnp.dot(a, b, preferred_element_type=F32)


def _dot_nt(a, b):
    return lax.dot_general(a, b, (((1,), (1,)), ((), ())), preferred_element_type=F32)


def _mixer_in_kernel(x_ref, win_ref, sg_ref, sb_ref, wsp_ref, bsp_ref, gg_ref,
                     ya_ref, q_ref, k_ref, v_ref):
    xb = x_ref[...].astype(BF16)

    def proj(lo, hi):
        return _dot(xb, win_ref[:, lo:hi])

    u = jax.nn.gelu(proj(0, SGU_WIDTH))
    vs = jax.nn.gelu(proj(SGU_WIDTH, 2 * SGU_WIDTH))
    r = lax.broadcasted_iota(jnp.int32, (CHUNK, CHUNK), 0)
    c = lax.broadcasted_iota(jnp.int32, (CHUNK, CHUNK), 1)
    causal = r >= c
    for g in range(SGU_GROUPS):
        gs = slice(g * GROUP_DIM, (g + 1) * GROUP_DIM)
        vn = _layer_norm(vs[:, gs], sg_ref[:, gs], sb_ref[:, gs]).astype(BF16)
        wc = jnp.where(causal, wsp_ref[g], 0.0).astype(BF16)
        for ch in range(TM_MIX // CHUNK):
            cs = slice(ch * CHUNK, (ch + 1) * CHUNK)
            gate = _dot(wc, vn[cs]) + bsp_ref[g]
            oa = u[cs, gs] * gate
            ms = jnp.mean(oa * oa, axis=-1, keepdims=True)
            ya_ref[cs, gs] = (oa * lax.rsqrt(ms + RMS_EPS) * gg_ref[:, gs]).astype(BF16)
    o = 2 * SGU_WIDTH
    q_ref[...] = (proj(o, o + SB_WIDTH) * (SB_HEAD_DIM ** -0.5)).astype(BF16)
    k_ref[...] = proj(o + SB_WIDTH, o + 2 * SB_WIDTH).astype(BF16)
    v_ref[...] = proj(o + 2 * SB_WIDTH, o + 3 * SB_WIDTH).astype(BF16)


def _mixer_in(x, w_in, sgu_g, sgu_b, w_sp, b_sp, grp_g_a):
    B, S, D = x.shape
    full = lambda shape: pl.BlockSpec(shape, lambda b, i: (0,) * len(shape))
    tile = lambda w: pl.BlockSpec((None, TM_MIX, w), lambda b, i: (b, i, 0))
    out = jax.ShapeDtypeStruct((B, S, SGU_WIDTH), BF16)
    return pl.pallas_call(
        _mixer_in_kernel,
        out_shape=(out, out, out, out),
        grid=(B, S // TM_MIX),
        in_specs=[tile(D), full(w_in.shape), full(sgu_g.shape), full(sgu_b.shape),
                  full(w_sp.shape), full(b_sp.shape), full(grp_g_a.shape)],
        out_specs=(tile(SGU_WIDTH),) * 4,
        compiler_params=pltpu.CompilerParams(
            dimension_semantics=("parallel", "parallel"), vmem_limit_bytes=VMEM_LIMIT),
        name="mixer_in",
    )(x, w_in, sgu_g, sgu_b, w_sp, b_sp, grp_g_a)


def _sb_attn_kernel(q_ref, k_ref, v_ref, gg_ref, o_ref):
    i = pl.program_id(2)
    lo = i * TQ_SB
    lane = lax.broadcasted_iota(jnp.int32, (1, LANES), 1)
    head0 = lane < SB_HEAD_DIM
    q_all = q_ref[...]
    r = lax.broadcasted_iota(jnp.int32, (2 * LANES, 2 * LANES), 0)
    c = lax.broadcasted_iota(jnp.int32, (2 * LANES, 2 * LANES), 1)
    after = jnp.where((r > c) & ((r // LANES) == (c // LANES)), 1.0, 0.0).astype(BF16)

    def split_heads(x):
        zx = jnp.zeros_like(x)
        return jnp.concatenate([jnp.where(head0, x, zx), jnp.where(head0, zx, x)], axis=0)

    def add_rows(x, r0, upd):
        new = x[r0:] + upd
        return jnp.concatenate([x[:r0], new], axis=0) if r0 else new

    def key_blocks(blocks, carry, acc):
        starts = [pl.multiple_of(s0, LANES) for _, s0, _ in blocks]
        k2s = [split_heads(k_ref[pl.ds(s, LANES), :]) for s in starts]
        v2s = [split_heads(v_ref[pl.ds(s, LANES), :]) for s in starts]
        zs = [_dot_nt(q_all[r0:], k2) for (r0, _, _), k2 in zip(blocks, k2s)]
        sps, masks = [], []
        for (r0, _, causal), s0, z in zip(blocks, starts, zs):
            sp = jnp.maximum(z, 0.0) + jnp.log(1.0 + jnp.exp2(jnp.abs(z) * (-LOG2E)))
            mask = None
            if causal:
                t_pos = lo + r0 + lax.broadcasted_iota(jnp.int32, z.shape, 0)
                s_pos = s0 + (lax.broadcasted_iota(jnp.int32, z.shape, 1) & (LANES - 1))
                mask = s_pos < t_pos
                sp = jnp.where(mask, sp, 0.0)
            sps.append(sp)
            masks.append(mask)
        his = [sp.astype(BF16) for sp in sps]
        los = [(sp - hi.astype(F32)).astype(BF16) for sp, hi in zip(sps, his)]
        laters = [_dot(hi, after) + _dot(lo_part, after) for hi, lo_part in zip(his, los)]
        weights = []
        for (r0, _, _), z, sp, later, mask in zip(blocks, zs, sps, laters, masks):
            a = jnp.exp(z - (sp + later + carry[r0:]))
            if mask is not None:
                a = jnp.where(mask, a, 0.0)
            weights.append(a.astype(BF16))
            rows = [jnp.broadcast_to(jnp.sum(sp[:, h * LANES:(h + 1) * LANES], axis=1, keepdims=True),
                                     (z.shape[0], LANES)) for h in range(2)]
            carry = add_rows(carry, r0, jnp.concatenate(rows, axis=1))
        for (r0, _, _), a, v2 in zip(blocks, weights, v2s):
            acc = add_rows(acc, r0, _dot(a, v2))
        return carry, acc

    nsub = TQ_SB // LANES
    carry = jnp.zeros((TQ_SB, 2 * LANES), F32)
    acc = jnp.zeros((TQ_SB, LANES), F32)
    carry, acc = key_blocks([(sb * LANES, lo + sb * LANES, True) for sb in reversed(range(nsub))],
                            carry, acc)

    def body(j, st):
        s0 = pl.multiple_of(lo - (j + 1) * TQ_SB, TQ_SB)
        return key_blocks([(0, s0 + sb * LANES, False) for sb in reversed(range(nsub))], *st)

    carry, acc = lax.fori_loop(0, i, body, (carry, acc))

    sq = acc * acc
    ss0 = jnp.sum(jnp.where(head0, sq, 0.0), axis=1, keepdims=True)
    ss1 = jnp.sum(jnp.where(head0, 0.0, sq), axis=1, keepdims=True)
    inv = 1.0 / SB_HEAD_DIM
    rs = jnp.where(head0, lax.rsqrt(ss0 * inv + RMS_EPS), lax.rsqrt(ss1 * inv + RMS_EPS))
    o_ref[...] = (acc * rs * gg_ref[...]).astype(BF16)


def _sb_attn(q, k, v, grp_g_b):
    B, S, W = q.shape
    pairs = W // LANES
    return pl.pallas_call(
        _sb_attn_kernel,
        out_shape=jax.ShapeDtypeStruct((B, S, W), BF16),
        grid=(B, pairs, S // TQ_SB),
        in_specs=[pl.BlockSpec((None, TQ_SB, LANES), lambda b, p, i: (b, i, p)),
                  pl.BlockSpec((None, S, LANES), lambda b, p, i: (b, 0, p)),
                  pl.BlockSpec((None, S, LANES), lambda b, p, i: (b, 0, p)),
                  pl.BlockSpec((1, LANES), lambda b, p, i: (0, p))],
        out_specs=pl.BlockSpec((None, TQ_SB, LANES), lambda b, p, i: (b, i, p)),
        compiler_params=pltpu.CompilerParams(
            dimension_semantics=("parallel", "parallel", "arbitrary"),
            vmem_limit_bytes=VMEM_LIMIT),
        name="sb_attn",
    )(q, k, v, grp_g_b)


def _kv_proj_kernel(mem_ref, wkv_ref, k_ref, v_ref):
    mb = mem_ref[...].astype(BF16)
    k_ref[...] = _dot(mb, wkv_ref[:, :D_MODEL]).astype(BF16)
    v_ref[...] = _dot(mb, wkv_ref[:, D_MODEL:]).astype(BF16)


def _kv_proj(mem, wkv):
    B, M, D = mem.shape
    out = jax.ShapeDtypeStruct((B, M, D), BF16)
    return pl.pallas_call(
        _kv_proj_kernel,
        out_shape=(out, out),
        grid=(B,),
        in_specs=[pl.BlockSpec((None, M, D), lambda b: (b, 0, 0)),
                  pl.BlockSpec(wkv.shape, lambda b: (0, 0))],
        out_specs=(pl.BlockSpec((None, M, D), lambda b: (b, 0, 0)),) * 2,
        compiler_params=pltpu.CompilerParams(
            dimension_semantics=("parallel",), vmem_limit_bytes=VMEM_LIMIT),
        name="kv_proj",
    )(mem, wkv)


def _pack_bf16_pairs(x):
    w = x.shape[1] // 2
    bits = lax.bitcast_convert_type(x.astype(BF16).astype(F32), jnp.uint32)
    return (bits[:, w:] & jnp.uint32(0xFFFF0000)) | (bits[:, :w] >> 16)


def _unpack_bf16_pairs(u):
    lo = lax.bitcast_convert_type(u << 16, F32).astype(BF16)
    hi = lax.bitcast_convert_type(u & jnp.uint32(0xFFFF0000), F32).astype(BF16)
    return lo, hi


def _post_mixer_kernel(x_ref, ya_ref, yb_ref, wout_ref, lng_ref, lnb_ref, wq_ref, km_ref, vm_ref,
                       wo_ref, wr_ref, br_ref,
                       x2_ref, x2p_ref, idx_ref, rank_ref, gate_ref, cnt_ref, carry_ref):
    first = jnp.logical_and(pl.program_id(0) == 0, pl.program_id(1) == 0)

    @pl.when(first)
    def _():
        carry_ref[...] = jnp.zeros_like(carry_ref)

    alpha = (2 * 2) ** 0.25
    mix = _dot(ya_ref[...], wout_ref[:SGU_WIDTH, :]) + _dot(yb_ref[...], wout_ref[SGU_WIDTH:, :])
    x1 = _layer_norm(alpha * x_ref[...] + mix, lng_ref[0:1, :], lnb_ref[0:1, :])

    qm = (_dot(x1.astype(BF16), wq_ref[...]) * (MEM_HEAD_DIM ** -0.5)).astype(BF16)
    heads = []
    for h in range(MEM_HEADS):
        hs = slice(h * MEM_HEAD_DIM, (h + 1) * MEM_HEAD_DIM)
        s = _dot_nt(qm[:, hs], km_ref[:, hs])
        e = jnp.exp(s - jnp.max(s, axis=-1, keepdims=True))
        p = (e / jnp.sum(e, axis=-1, keepdims=True)).astype(BF16)
        heads.append(_dot(p, vm_ref[:, hs]).astype(BF16))
    o = jnp.concatenate(heads, axis=1)
    x2 = _layer_norm(alpha * x1 + _dot(o, wo_ref[...]), lng_ref[1:2, :], lnb_ref[1:2, :])
    x2_ref[...] = x2
    x2p_ref[...] = _pack_bf16_pairs(x2)

    xh = x2.astype(BF16)
    xl = (x2 - xh.astype(F32)).astype(BF16)
    wr = wr_ref[...]
    wh = wr.astype(BF16)
    wl = (wr - wh.astype(F32)).astype(BF16)
    logits = _dot(xh, wh) + (_dot(xl, wh) + _dot(xh, wl)) + br_ref[...]

    tm = logits.shape[0]
    lane = lax.broadcasted_iota(jnp.int32, (tm, N_EXPERTS), 1).astype(F32)
    work = logits
    vals, idxs = [], []
    chosen = jnp.zeros((tm, N_EXPERTS), F32)
    for _ in range(TOP_K):
        m = jnp.max(work, axis=-1, keepdims=True)
        ix = jnp.min(jnp.where(work == m, lane, float(N_EXPERTS)), axis=-1, keepdims=True)
        sel = lane == ix
        work = jnp.where(sel, -jnp.inf, work)
        chosen = jnp.where(sel, 1.0, chosen)
        vals.append(m)
        idxs.append(ix)
    ev = [jnp.exp(v - vals[0]) for v in vals]
    den = ev[0] + ev[1] + ev[2] + ev[3]

    r = lax.broadcasted_iota(jnp.int32, (tm, tm), 0)
    c = lax.broadcasted_iota(jnp.int32, (tm, tm), 1)
    before = jnp.where(r > c, 1.0, 0.0).astype(BF16)
    running = _dot(before, chosen.astype(BF16)) + carry_ref[...]
    ranks = [jnp.sum(jnp.where(lane == ix, running, 0.0), axis=-1, keepdims=True) for ix in idxs]
    carry_ref[...] = carry_ref[...] + jnp.sum(chosen, axis=0, keepdims=True)
    cnt_ref[...] = carry_ref[...]

    l4 = lax.broadcasted_iota(jnp.int32, (tm, TOP_K), 1)

    def cols(xs):
        return jnp.where(l4 == 0, xs[0], jnp.where(l4 == 1, xs[1], jnp.where(l4 == 2, xs[2], xs[3])))

    idx_ref[...] = cols(idxs).astype(jnp.int32)
    rank_ref[...] = cols(ranks).astype(jnp.int32)
    gate_ref[...] = cols([e / den for e in ev])


def _post_mixer(x, ya, yb, w_out, ln_g, ln_b, wq, k_mem, v_mem, wo, w_router, b_router):
    B, S, D = x.shape
    M = k_mem.shape[1]
    full = lambda a: pl.BlockSpec(a.shape, lambda b, i: (0,) * a.ndim)
    tile = lambda w: pl.BlockSpec((None, TM_POST, w), lambda b, i: (b, i, 0))
    memb = pl.BlockSpec((None, M, D), lambda b, i: (b, 0, 0))
    sds = jax.ShapeDtypeStruct
    return pl.pallas_call(
        _post_mixer_kernel,
        out_shape=(sds((B, S, D), F32), sds((B, S, D // 2), jnp.uint32),
                   sds((B, S, TOP_K), jnp.int32), sds((B, S, TOP_K), jnp.int32),
                   sds((B, S, TOP_K), F32), sds((1, N_EXPERTS), F32)),
        grid=(B, S // TM_POST),
        in_specs=[tile(D), tile(SGU_WIDTH), tile(SB_WIDTH), full(w_out), full(ln_g), full(ln_b),
                  full(wq), memb, memb, full(wo), full(w_router), full(b_router)],
        out_specs=(tile(D), tile(D // 2), tile(TOP_K), tile(TOP_K), tile(TOP_K),
                   pl.BlockSpec((1, N_EXPERTS), lambda b, i: (0, 0))),
        scratch_shapes=[pltpu.VMEM((1, N_EXPERTS), F32)],
        compiler_params=pltpu.CompilerParams(
            dimension_semantics=("arbitrary", "arbitrary"), vmem_limit_bytes=VMEM_LIMIT),
        name="post_mixer",
    )(x, ya, yb, w_out, ln_g, ln_b, wq, k_mem, v_mem, wo, w_router, b_router)


def _row_copy(src_ref, src_row, dst_ref, dst_row, sem):
    return pltpu.make_async_copy(src_ref.at[pl.ds(src_row, 1)], dst_ref.at[pl.ds(dst_row, 1)], sem)


def _dest_rows_kernel(pstart_ref, idx_ref, rank_ref, dest_ref):
    idx = idx_ref[...]
    base = jnp.zeros_like(idx)
    for e in range(N_EXPERTS):
        base = jnp.where(idx == e, pstart_ref[e], base)
    dest_ref[...] = base + rank_ref[...]


def _dest_rows(pstart, idx_flat, rank_flat):
    n = idx_flat.shape[0]
    shape2d = (n // LANES, LANES)
    full = pl.BlockSpec(shape2d, lambda i, ps: (0, 0))
    dest = pl.pallas_call(
        _dest_rows_kernel,
        out_shape=jax.ShapeDtypeStruct(shape2d, jnp.int32),
        grid_spec=pltpu.PrefetchScalarGridSpec(
            num_scalar_prefetch=1, grid=(1,), in_specs=[full, full], out_specs=full),
        name="dest_rows",
    )(pstart, idx_flat.reshape(shape2d), rank_flat.reshape(shape2d))
    return dest.reshape(n)


def _dispatch_kernel(bend_ref, dest_ref, x2p_ref, rows_ref, zbuf, sem, zsem):
    @pl.when(pl.program_id(0) == 0)
    def _():
        zbuf[...] = jnp.zeros_like(zbuf)

        def tail_fill(e):
            last = pl.multiple_of((bend_ref[e] - 1) * BLK_E, BLK_E)
            return pltpu.make_async_copy(zbuf, rows_ref.at[pl.ds(last, BLK_E)], zsem)

        def nonempty(e):
            return bend_ref[e] > (bend_ref[e - 1] if e else 0)

        for e in range(N_EXPERTS):
            pl.when(nonempty(e))(lambda e=e: tail_fill(e).start())
        for e in range(N_EXPERTS):
            pl.when(nonempty(e))(lambda e=e: tail_fill(e).wait())

        def unused_fill(b):
            return pltpu.make_async_copy(
                zbuf, rows_ref.at[pl.ds(pl.multiple_of(b * BLK_E, BLK_E), BLK_E)], zsem)

        used, total = bend_ref[N_EXPERTS - 1], rows_ref.shape[0] // BLK_E
        lax.fori_loop(used, total, lambda b, c: (unused_fill(b).start(), c)[1], 0)
        lax.fori_loop(used, total, lambda b, c: (unused_fill(b).wait(), c)[1], 0)

    def start(t, carry):
        for k in range(TOP_K):
            _row_copy(x2p_ref, t, rows_ref, dest_ref[t * TOP_K + k], sem).start(priority=k % 2)
        return carry

    lax.fori_loop(0, TD_ROWS, start, 0, unroll=DMA_UNROLL)
    for k in range(TOP_K):
        pltpu.make_async_copy(x2p_ref, rows_ref.at[pl.ds(0, TD_ROWS)], sem).wait()


def _dispatch(bend, dest_flat, x2p, n_rows):
    N, W = x2p.shape
    smem_blk = pl.BlockSpec((TD_ROWS * TOP_K,), lambda i, be: (i,), memory_space=pltpu.SMEM)
    grid_spec = pltpu.PrefetchScalarGridSpec(
        num_scalar_prefetch=1,
        grid=(N // TD_ROWS,),
        in_specs=[smem_blk, pl.BlockSpec((TD_ROWS, W), lambda i, be: (i, 0))],
        out_specs=pl.BlockSpec(memory_space=pl.ANY),
        scratch_shapes=[pltpu.VMEM((BLK_E, W), jnp.uint32), pltpu.SemaphoreType.DMA(()),
                        pltpu.SemaphoreType.DMA(())],
    )
    return pl.pallas_call(
        _dispatch_kernel,
        out_shape=jax.ShapeDtypeStruct((n_rows, W), jnp.uint32),
        grid_spec=grid_spec,
        compiler_params=pltpu.CompilerParams(
            dimension_semantics=("arbitrary",), has_side_effects=True),
        name="dispatch",
    )(bend, dest_flat, x2p)


def _experts_kernel(layer, be_ref, first_ref, nvalid_ref, next_ref, slot_ref,
                    rows_ref, bgu_ref, bd_ref, wgu_hbm, wd_hbm,
                    y_ref, wgu_f32, wd_f32, wgu_bf, wd_bf, sem):
    i = pl.program_id(0)

    def fetch(e, slot):
        return (pltpu.make_async_copy(wgu_hbm.at[layer, e], wgu_f32.at[slot], sem.at[0, slot]),
                pltpu.make_async_copy(wd_hbm.at[layer, e], wd_f32.at[slot], sem.at[1, slot]))

    @pl.when(i < nvalid_ref[0])
    def _():
        @pl.when(first_ref[i] == 1)
        def _():
            slot = slot_ref[i]

            @pl.when(i == 0)
            def _():
                for c in fetch(be_ref[i], slot):
                    c.start()

            for c in fetch(be_ref[i], slot):
                c.wait()
            wgu_bf[...] = wgu_f32[slot].astype(BF16)
            wd_bf[...] = wd_f32[slot].astype(BF16)

            @pl.when(next_ref[i] >= 0)
            def _():
                for c in fetch(next_ref[i], 1 - slot):
                    c.start()

        lo, hi = _unpack_bf16_pairs(rows_ref[...])
        half = D_MODEL // 2
        hgu = _dot(lo, wgu_bf[:half, :]) + _dot(hi, wgu_bf[half:, :]) + bgu_ref[...]
        g = jnp.minimum(hgu[:, :D_MODEL], SWIGLU_LIMIT)
        u = jnp.clip(hgu[:, D_MODEL:], -SWIGLU_LIMIT, SWIGLU_LIMIT)
        a = (u + 1.0) * (g * jax.nn.sigmoid(SWIGLU_ALPHA * g))
        y_ref[...] = _dot(a.astype(BF16), wd_bf[...]) + bd_ref[...]

    @pl.when(i >= nvalid_ref[0])
    def _():
        y_ref[...] = jnp.zeros_like(y_ref)


def _experts(layer, plan, rows, w_gu, b_gu, w_down, b_down):
    P, W = rows.shape
    L, E, D, F2 = w_gu.shape
    nb = P // BLK_E

    def row_map(i, be, bf, nv, nx, sl):
        return (jnp.minimum(i, nv[0] - 1), 0)

    def b_map(i, be, bf, nv, nx, sl):
        return (layer, be[i], 0, 0)

    grid_spec = pltpu.PrefetchScalarGridSpec(
        num_scalar_prefetch=5,
        grid=(nb,),
        in_specs=[pl.BlockSpec((BLK_E, W), row_map),
                  pl.BlockSpec((None, None, 1, F2), b_map),
                  pl.BlockSpec((None, None, 1, D), b_map),
                  pl.BlockSpec(memory_space=pl.ANY),
                  pl.BlockSpec(memory_space=pl.ANY)],
        out_specs=pl.BlockSpec((BLK_E, D), lambda i, *_: (i, 0)),
        scratch_shapes=[pltpu.VMEM((2, D, F2), F32), pltpu.VMEM((2, F2 // 2, D), F32),
                        pltpu.VMEM((D, F2), BF16), pltpu.VMEM((F2 // 2, D), BF16),
                        pltpu.SemaphoreType.DMA((2, 2))],
    )
    return pl.pallas_call(
        functools.partial(_experts_kernel, layer),
        out_shape=jax.ShapeDtypeStruct((P, D), F32),
        grid_spec=grid_spec,
        compiler_params=pltpu.CompilerParams(
            dimension_semantics=("arbitrary",), vmem_limit_bytes=VMEM_LIMIT),
        name="experts",
    )(plan.block_e, plan.block_first, plan.nvalid, plan.next_e, plan.slot, rows,
      b_gu.reshape(L, E, 1, F2), b_down.reshape(L, E, 1, D), w_gu, w_down)


def _combine_kernel(dest_ref, gate_ref, x2_ref, lng_ref, lnb_ref, y_ref, o_ref, ybuf, sem):
    def start(t, carry):
        for k in range(TOP_K):
            _row_copy(y_ref, dest_ref[t * TOP_K + k], ybuf.at[k], t, sem).start(priority=k % 2)
        return carry

    lax.fori_loop(0, TD_ROWS, start, 0, unroll=DMA_UNROLL)
    for k in range(TOP_K):
        pltpu.make_async_copy(y_ref.at[pl.ds(0, TD_ROWS)], ybuf.at[k], sem).wait()

    gates = gate_ref[...]
    ff = gates[:, 0:1] * ybuf[0]
    for k in range(1, TOP_K):
        ff = ff + gates[:, k:k + 1] * ybuf[k]
    alpha = (2 * 2) ** 0.25
    o_ref[...] = _layer_norm(alpha * x2_ref[...] + ff, lng_ref[2:3, :], lnb_ref[2:3, :])


def _combine(dest_flat, gates, x2, ln_g, ln_b, y):
    N, D = x2.shape
    return pl.pallas_call(
        _combine_kernel,
        out_shape=jax.ShapeDtypeStruct((N, D), F32),
        grid=(N // TD_ROWS,),
        in_specs=[pl.BlockSpec((TD_ROWS * TOP_K,), lambda i: (i,), memory_space=pltpu.SMEM),
                  pl.BlockSpec((TD_ROWS, TOP_K), lambda i: (i, 0)),
                  pl.BlockSpec((TD_ROWS, D), lambda i: (i, 0)),
                  pl.BlockSpec(ln_g.shape, lambda i: (0, 0)),
                  pl.BlockSpec(ln_b.shape, lambda i: (0, 0)),
                  pl.BlockSpec(memory_space=pl.ANY)],
        out_specs=pl.BlockSpec((TD_ROWS, D), lambda i: (i, 0)),
        scratch_shapes=[pltpu.VMEM((TOP_K, TD_ROWS, D), F32), pltpu.SemaphoreType.DMA(())],
        compiler_params=pltpu.CompilerParams(
            dimension_semantics=("arbitrary",), vmem_limit_bytes=VMEM_LIMIT),
        name="combine",
    )(dest_flat, gates, x2, ln_g, ln_b, y)


class _Plan(NamedTuple):
    pstart: jax.Array
    bend: jax.Array
    nvalid: jax.Array
    block_e: jax.Array
    block_first: jax.Array
    next_e: jax.Array
    slot: jax.Array


def _routing_plan(counts, nb):
    counts = counts.reshape(N_EXPERTS).astype(jnp.int32)
    nblk = (counts + BLK_E - 1) // BLK_E
    bend = jnp.cumsum(nblk)
    pstart = (bend - nblk) * BLK_E
    nvalid = bend[-1:]
    blocks = jnp.minimum(jnp.arange(nb, dtype=jnp.int32), nvalid[0] - 1)
    block_e = jnp.sum((blocks[:, None] >= bend[None, :]).astype(jnp.int32), axis=1)
    block_e = jnp.minimum(block_e, N_EXPERTS - 1)
    block_first = jnp.concatenate(
        [jnp.ones((1,), jnp.int32), (block_e[1:] != block_e[:-1]).astype(jnp.int32)])
    experts = jnp.arange(N_EXPERTS, dtype=jnp.int32)
    at_or_after = lax.cummin(jnp.where(nblk > 0, experts, N_EXPERTS), reverse=True)
    after = jnp.concatenate([at_or_after[1:], jnp.full((1,), N_EXPERTS, jnp.int32)])
    next_e = jnp.where(after < N_EXPERTS, after, -1)[block_e]
    slot = (jnp.cumsum(block_first) - 1) & 1
    return _Plan(pstart, bend, nvalid, block_e, block_first, next_e.astype(jnp.int32),
                 slot.astype(jnp.int32))


def _layer(layer, x, mem, w_in, sgu_g, sgu_b, w_sp, b_sp, grp_g, w_out, wq, wkv, wo,
           w_router, b_router, w_gu, b_gu, w_down, b_down, ln_g, ln_b):
    B, S, D = x.shape
    N = B * S
    row = lambda a: a.reshape(1, -1)

    ya, q, k, v = _mixer_in(x, w_in.astype(BF16), row(sgu_g), row(sgu_b), w_sp,
                            b_sp.reshape(SGU_GROUPS, CHUNK, 1), row(grp_g[:SGU_WIDTH]))
    yb = _sb_attn(q, k, v, row(grp_g[SGU_WIDTH:]))
    k_mem, v_mem = _kv_proj(mem, wkv.astype(BF16))
    x2, x2p, idx, rank, gates, counts = _post_mixer(
        x, ya, yb, w_out.astype(BF16), ln_g, ln_b, wq.astype(BF16), k_mem, v_mem,
        wo.astype(BF16), w_router, row(b_router))

    nb = (N * TOP_K) // BLK_E + N_EXPERTS
    plan = _routing_plan(counts, nb)
    dest = _dest_rows(plan.pstart, idx.reshape(-1), rank.reshape(-1))
    rows = _dispatch(plan.bend, dest, x2p.reshape(N, D // 2), nb * BLK_E)
    y = _experts(layer, plan, rows, w_gu, b_gu, w_down, b_down)
    out = _combine(dest, gates.reshape(N, TOP_K), x2.reshape(N, D), ln_g, ln_b, y)
    return out.reshape(B, S, D)


def kernel(x, mem, w_in, sgu_g, sgu_b, w_sp, b_sp, grp_g, w_out, wq_mem, wkv_mem, wo_mem,
           w_router, b_router, w_gu, b_gu, w_down, b_down, ln_g, ln_b):
    for l in range(w_in.shape[0]):
        x = _layer(l, x, mem, w_in[l], sgu_g[l], sgu_b[l], w_sp[l], b_sp[l], grp_g[l], w_out[l],
                   wq_mem[l], wkv_mem[l], wo_mem[l], w_router[l], b_router[l], w_gu, b_gu,
                   w_down, b_down, ln_g[l], ln_b[l])
    return x
```

```python
import functools
from typing import NamedTuple

import jax
import jax.numpy as jnp
from jax import lax
from jax.experimental import pallas as pl
from jax.experimental.pallas import tpu as pltpu

F32 = jnp.float32
BF16 = jnp.bfloat16

D_MODEL = 1024
SGU_WIDTH = 512
SGU_GROUPS = 4
GROUP_DIM = 128
CHUNK = 128
SB_WIDTH = 512
SB_HEADS = 8
SB_HEAD_DIM = 64
MEM_HEADS = 4
MEM_HEAD_DIM = 256
N_EXPERTS = 32
TOP_K = 4
SWIGLU_LIMIT = 7.0
SWIGLU_ALPHA = 1.702
LN_EPS = 1e-5
RMS_EPS = 1e-6
LOG2E = 1.4426950408889634

LANES = 128
TM_MIX = 512
TQ_SB = 512
TM_POST = 512
TD_ROWS = 256
BLK_E = 512
DMA_UNROLL = 4
VMEM_LIMIT = 56 * 1024 * 1024


def _layer_norm(x, g, b):
    mu = jnp.mean(x, axis=-1, keepdims=True)
    xc = x - mu
    var = jnp.mean(xc * xc, axis=-1, keepdims=True)
    return xc * lax.rsqrt(var + LN_EPS) * g + b


def _dot(a, b):
    return jnp.dot(a, b, preferred_element_type=F32)


def _dot_nt(a, b):
    return lax.dot_general(a, b, (((1,), (1,)), ((), ())), preferred_element_type=F32)


def _mixer_in_kernel(x_ref, win_ref, sg_ref, sb_ref, wsp_ref, bsp_ref, gg_ref,
                     ya_ref, q_ref, k_ref, v_ref):
    xb = x_ref[...].astype(BF16)

    def proj(lo, hi):
        return _dot(xb, win_ref[:, lo:hi])

    u = jax.nn.gelu(proj(0, SGU_WIDTH))
    vs = jax.nn.gelu(proj(SGU_WIDTH, 2 * SGU_WIDTH))
    r = lax.broadcasted_iota(jnp.int32, (CHUNK, CHUNK), 0)
    c = lax.broadcasted_iota(jnp.int32, (CHUNK, CHUNK), 1)
    causal = r >= c
    for g in range(SGU_GROUPS):
        gs = slice(g * GROUP_DIM, (g + 1) * GROUP_DIM)
        vn = _layer_norm(vs[:, gs], sg_ref[:, gs], sb_ref[:, gs]).astype(BF16)
        wc = jnp.where(causal, wsp_ref[g], 0.0).astype(BF16)
        for ch in range(TM_MIX // CHUNK):
            cs = slice(ch * CHUNK, (ch + 1) * CHUNK)
            gate = _dot(wc, vn[cs]) + bsp_ref[g]
            oa = u[cs, gs] * gate
            ms = jnp.mean(oa * oa, axis=-1, keepdims=True)
            ya_ref[cs, gs] = (oa * lax.rsqrt(ms + RMS_EPS) * gg_ref[:, gs]).astype(BF16)
    o = 2 * SGU_WIDTH
    q_ref[...] = (proj(o, o + SB_WIDTH) * (SB_HEAD_DIM ** -0.5)).astype(BF16)
    k_ref[...] = proj(o + SB_WIDTH, o + 2 * SB_WIDTH).astype(BF16)
    v_ref[...] = proj(o + 2 * SB_WIDTH, o + 3 * SB_WIDTH).astype(BF16)


def _mixer_in(x, w_in, sgu_g, sgu_b, w_sp, b_sp, grp_g_a):
    B, S, D = x.shape
    full = lambda shape: pl.BlockSpec(shape, lambda b, i: (0,) * len(shape))
    tile = lambda w: pl.BlockSpec((None, TM_MIX, w), lambda b, i: (b, i, 0))
    out = jax.ShapeDtypeStruct((B, S, SGU_WIDTH), BF16)
    return pl.pallas_call(
        _mixer_in_kernel,
        out_shape=(out, out, out, out),
        grid=(B, S // TM_MIX),
        in_specs=[tile(D), full(w_in.shape), full(sgu_g.shape), full(sgu_b.shape),
                  full(w_sp.shape), full(b_sp.shape), full(grp_g_a.shape)],
        out_specs=(tile(SGU_WIDTH),) * 4,
        compiler_params=pltpu.CompilerParams(
            dimension_semantics=("parallel", "parallel"), vmem_limit_bytes=VMEM_LIMIT),
        name="mixer_in",
    )(x, w_in, sgu_g, sgu_b, w_sp, b_sp, grp_g_a)


def _sb_attn_kernel(q_ref, k_ref, v_ref, gg_ref, o_ref):
    i = pl.program_id(2)
    lo = i * TQ_SB
    lane = lax.broadcasted_iota(jnp.int32, (1, LANES), 1)
    head0 = lane < SB_HEAD_DIM
    q_all = q_ref[...]
    r = lax.broadcasted_iota(jnp.int32, (2 * LANES, 2 * LANES), 0)
    c = lax.broadcasted_iota(jnp.int32, (2 * LANES, 2 * LANES), 1)
    after = jnp.where((r > c) & ((r // LANES) == (c // LANES)), 1.0, 0.0).astype(BF16)

    def split_heads(x):
        zx = jnp.zeros_like(x)
        return jnp.concatenate([jnp.where(head0, x, zx), jnp.where(head0, zx, x)], axis=0)

    def add_rows(x, r0, upd):
        new = x[r0:] + upd
        return jnp.concatenate([x[:r0], new], axis=0) if r0 else new

    def key_blocks(blocks, carry, acc):
        starts = [pl.multiple_of(s0, LANES) for _, s0, _ in blocks]
        k2s = [split_heads(k_ref[pl.ds(s, LANES), :]) for s in starts]
        v2s = [split_heads(v_ref[pl.ds(s, LANES), :]) for s in starts]
        zs = [_dot_nt(q_all[r0:], k2) for (r0, _, _), k2 in zip(blocks, k2s)]
        sps, masks = [], []
        for (r0, _, causal), s0, z in zip(blocks, starts, zs):
            sp = jnp.maximum(z, 0.0) + jnp.log(1.0 + jnp.exp2(jnp.abs(z) * (-LOG2E)))
            mask = None
            if causal:
                t_pos = lo + r0 + lax.broadcasted_iota(jnp.int32, z.shape, 0)
                s_pos = s0 + (lax.broadcasted_iota(jnp.int32, z.shape, 1) & (LANES - 1))
                mask = s_pos < t_pos
                sp = jnp.where(mask, sp, 0.0)
            sps.append(sp)
            masks.append(mask)
        his = [sp.astype(BF16) for sp in sps]
        los = [(sp - hi.astype(F32)).astype(BF16) for sp, hi in zip(sps, his)]
        laters = [_dot(hi, after) + _dot(lo_part, after) for hi, lo_part in zip(his, los)]
        weights = []
        for (r0, _, _), z, sp, later, mask in zip(blocks, zs, sps, laters, masks):
            a = jnp.exp(z - (sp + later + carry[r0:]))
            if mask is not None:
                a = jnp.where(mask, a, 0.0)
            weights.append(a.astype(BF16))
            rows = [jnp.broadcast_to(jnp.sum(sp[:, h * LANES:(h + 1) * LANES], axis=1, keepdims=True),
                                     (z.shape[0], LANES)) for h in range(2)]
            carry = add_rows(carry, r0, jnp.concatenate(rows, axis=1))
        for (r0, _, _), a, v2 in zip(blocks, weights, v2s):
            acc = add_rows(acc, r0, _dot(a, v2))
        return carry, acc

    nsub = TQ_SB // LANES
    carry = jnp.zeros((TQ_SB, 2 * LANES), F32)
    acc = jnp.zeros((TQ_SB, LANES), F32)
    carry, acc = key_blocks([(sb * LANES, lo + sb * LANES, True) for sb in reversed(range(nsub))],
                            carry, acc)

    def body(j, st):
        s0 = pl.multiple_of(lo - (j + 1) * TQ_SB, TQ_SB)
        return key_blocks([(0, s0 + sb * LANES, False) for sb in reversed(range(nsub))], *st)

    carry, acc = lax.fori_loop(0, i, body, (carry, acc))

    sq = acc * acc
    ss0 = jnp.sum(jnp.where(head0, sq, 0.0), axis=1, keepdims=True)
    ss1 = jnp.sum(jnp.where(head0, 0.0, sq), axis=1, keepdims=True)
    inv = 1.0 / SB_HEAD_DIM
    rs = jnp.where(head0, lax.rsqrt(ss0 * inv + RMS_EPS), lax.rsqrt(ss1 * inv + RMS_EPS))
    o_ref[...] = (acc * rs * gg_ref[...]).astype(BF16)


def _sb_attn(q, k, v, grp_g_b):
    B, S, W = q.shape
    pairs = W // LANES
    return pl.pallas_call(
        _sb_attn_kernel,
        out_shape=jax.ShapeDtypeStruct((B, S, W), BF16),
        grid=(B, pairs, S // TQ_SB),
        in_specs=[pl.BlockSpec((None, TQ_SB, LANES), lambda b, p, i: (b, i, p)),
                  pl.BlockSpec((None, S, LANES), lambda b, p, i: (b, 0, p)),
                  pl.BlockSpec((None, S, LANES), lambda b, p, i: (b, 0, p)),
                  pl.BlockSpec((1, LANES), lambda b, p, i: (0, p))],
        out_specs=pl.BlockSpec((None, TQ_SB, LANES), lambda b, p, i: (b, i, p)),
        compiler_params=pltpu.CompilerParams(
            dimension_semantics=("parallel", "parallel", "arbitrary"),
            vmem_limit_bytes=VMEM_LIMIT),
        name="sb_attn",
    )(q, k, v, grp_g_b)


def _kv_proj_kernel(mem_ref, wkv_ref, k_ref, v_ref):
    mb = mem_ref[...].astype(BF16)
    k_ref[...] = _dot(mb, wkv_ref[:, :D_MODEL]).astype(BF16)
    v_ref[...] = _dot(mb, wkv_ref[:, D_MODEL:]).astype(BF16)


def _kv_proj(mem, wkv):
    B, M, D = mem.shape
    out = jax.ShapeDtypeStruct((B, M, D), BF16)
    return pl.pallas_call(
        _kv_proj_kernel,
        out_shape=(out, out),
        grid=(B,),
        in_specs=[pl.BlockSpec((None, M, D), lambda b: (b, 0, 0)),
                  pl.BlockSpec(wkv.shape, lambda b: (0, 0))],
        out_specs=(pl.BlockSpec((None, M, D), lambda b: (b, 0, 0)),) * 2,
        compiler_params=pltpu.CompilerParams(
            dimension_semantics=("parallel",), vmem_limit_bytes=VMEM_LIMIT),
        name="kv_proj",
    )(mem, wkv)


def _pack_bf16_pairs(x):
    w = x.shape[1] // 2
    bits = lax.bitcast_convert_type(x.astype(BF16).astype(F32), jnp.uint32)
    return (bits[:, w:] & jnp.uint32(0xFFFF0000)) | (bits[:, :w] >> 16)


def _unpack_bf16_pairs(u):
    lo = lax.bitcast_convert_type(u << 16, F32).astype(BF16)
    hi = lax.bitcast_convert_type(u & jnp.uint32(0xFFFF0000), F32).astype(BF16)
    return lo, hi


def _post_mixer_kernel(x_ref, ya_ref, yb_ref, wout_ref, lng_ref, lnb_ref, wq_ref, km_ref, vm_ref,
                       wo_ref, wr_ref, br_ref,
                       x2_ref, x2p_ref, idx_ref, rank_ref, gate_ref, cnt_ref, carry_ref):
    first = jnp.logical_and(pl.program_id(0) == 0, pl.program_id(1) == 0)

    @pl.when(first)
    def _():
        carry_ref[...] = jnp.zeros_like(carry_ref)

    alpha = (2 * 2) ** 0.25
    mix = _dot(ya_ref[...], wout_ref[:SGU_WIDTH, :]) + _dot(yb_ref[...], wout_ref[SGU_WIDTH:, :])
    x1 = _layer_norm(alpha * x_ref[...] + mix, lng_ref[0:1, :], lnb_ref[0:1, :])

    qm = (_dot(x1.astype(BF16), wq_ref[...]) * (MEM_HEAD_DIM ** -0.5)).astype(BF16)
    heads = []
    for h in range(MEM_HEADS):
        hs = slice(h * MEM_HEAD_DIM, (h + 1) * MEM_HEAD_DIM)
        s = _dot_nt(qm[:, hs], km_ref[:, hs])
        e = jnp.exp(s - jnp.max(s, axis=-1, keepdims=True))
        p = (e / jnp.sum(e, axis=-1, keepdims=True)).astype(BF16)
        heads.append(_dot(p, vm_ref[:, hs]).astype(BF16))
    o = jnp.concatenate(heads, axis=1)
    x2 = _layer_norm(alpha * x1 + _dot(o, wo_ref[...]), lng_ref[1:2, :], lnb_ref[1:2, :])
    x2_ref[...] = x2
    x2p_ref[...] = _pack_bf16_pairs(x2)

    xh = x2.astype(BF16)
    xl = (x2 - xh.astype(F32)).astype(BF16)
    wr = wr_ref[...]
    wh = wr.astype(BF16)
    wl = (wr - wh.astype(F32)).astype(BF16)
    logits = _dot(xh, wh) + (_dot(xl, wh) + _dot(xh, wl)) + br_ref[...]

    tm = logits.shape[0]
    lane = lax.broadcasted_iota(jnp.int32, (tm, N_EXPERTS), 1).astype(F32)
    work = logits
    vals, idxs = [], []
    chosen = jnp.zeros((tm, N_EXPERTS), F32)
    for _ in range(TOP_K):
        m = jnp.max(work, axis=-1, keepdims=True)
        ix = jnp.min(jnp.where(work == m, lane, float(N_EXPERTS)), axis=-1, keepdims=True)
        sel = lane == ix
        work = jnp.where(sel, -jnp.inf, work)
        chosen = jnp.where(sel, 1.0, chosen)
        vals.append(m)
        idxs.append(ix)
    ev = [jnp.exp(v - vals[0]) for v in vals]
    den = ev[0] + ev[1] + ev[2] + ev[3]

    r = lax.broadcasted_iota(jnp.int32, (tm, tm), 0)
    c = lax.broadcasted_iota(jnp.int32, (tm, tm), 1)
    before = jnp.where(r > c, 1.0, 0.0).astype(BF16)
    running = _dot(before, chosen.astype(BF16)) + carry_ref[...]
    ranks = [jnp.sum(jnp.where(lane == ix, running, 0.0), axis=-1, keepdims=True) for ix in idxs]
    carry_ref[...] = carry_ref[...] + jnp.sum(chosen, axis=0, keepdims=True)
    cnt_ref[...] = carry_ref[...]

    l4 = lax.broadcasted_iota(jnp.int32, (tm, TOP_K), 1)

    def cols(xs):
        return jnp.where(l4 == 0, xs[0], jnp.where(l4 == 1, xs[1], jnp.where(l4 == 2, xs[2], xs[3])))

    idx_ref[...] = cols(idxs).astype(jnp.int32)
    rank_ref[...] = cols(ranks).astype(jnp.int32)
    gate_ref[...] = cols([e / den for e in ev])


def _post_mixer(x, ya, yb, w_out, ln_g, ln_b, wq, k_mem, v_mem, wo, w_router, b_router):
    B, S, D = x.shape
    M = k_mem.shape[1]
    full = lambda a: pl.BlockSpec(a.shape, lambda b, i: (0,) * a.ndim)
    tile = lambda w: pl.BlockSpec((None, TM_POST, w), lambda b, i: (b, i, 0))
    memb = pl.BlockSpec((None, M, D), lambda b, i: (b, 0, 0))
    sds = jax.ShapeDtypeStruct
    return pl.pallas_call(
        _post_mixer_kernel,
        out_shape=(sds((B, S, D), F32), sds((B, S, D // 2), jnp.uint32),
                   sds((B, S, TOP_K), jnp.int32), sds((B, S, TOP_K), jnp.int32),
                   sds((B, S, TOP_K), F32), sds((1, N_EXPERTS), F32)),
        grid=(B, S // TM_POST),
        in_specs=[tile(D), tile(SGU_WIDTH), tile(SB_WIDTH), full(w_out), full(ln_g), full(ln_b),
                  full(wq), memb, memb, full(wo), full(w_router), full(b_router)],
        out_specs=(tile(D), tile(D // 2), tile(TOP_K), tile(TOP_K), tile(TOP_K),
                   pl.BlockSpec((1, N_EXPERTS), lambda b, i: (0, 0))),
        scratch_shapes=[pltpu.VMEM((1, N_EXPERTS), F32)],
        compiler_params=pltpu.CompilerParams(
            dimension_semantics=("arbitrary", "arbitrary"), vmem_limit_bytes=VMEM_LIMIT),
        name="post_mixer",
    )(x, ya, yb, w_out, ln_g, ln_b, wq, k_mem, v_mem, wo, w_router, b_router)


def _row_copy(src_ref, src_row, dst_ref, dst_row, sem):
    return pltpu.make_async_copy(src_ref.at[pl.ds(src_row, 1)], dst_ref.at[pl.ds(dst_row, 1)], sem)


def _dest_rows_kernel(pstart_ref, idx_ref, rank_ref, dest_ref):
    idx = idx_ref[...]
    base = jnp.zeros_like(idx)
    for e in range(N_EXPERTS):
        base = jnp.where(idx == e, pstart_ref[e], base)
    dest_ref[...] = base + rank_ref[...]


def _dest_rows(pstart, idx_flat, rank_flat):
    n = idx_flat.shape[0]
    shape2d = (n // LANES, LANES)
    full = pl.BlockSpec(shape2d, lambda i, ps: (0, 0))
    dest = pl.pallas_call(
        _dest_rows_kernel,
        out_shape=jax.ShapeDtypeStruct(shape2d, jnp.int32),
        grid_spec=pltpu.PrefetchScalarGridSpec(
            num_scalar_prefetch=1, grid=(1,), in_specs=[full, full], out_specs=full),
        name="dest_rows",
    )(pstart, idx_flat.reshape(shape2d), rank_flat.reshape(shape2d))
    return dest.reshape(n)


def _dispatch_kernel(bend_ref, dest_ref, x2p_hbm, rows_ref, zbuf, sem, zsem):
    i = pl.program_id(0)
    last = pl.num_programs(0) - 1

    @pl.when(i == 0)
    def _():
        zbuf[...] = jnp.zeros_like(zbuf)

        def tail_fill(e):
            last = pl.multiple_of((bend_ref[e] - 1) * BLK_E, BLK_E)
            return pltpu.make_async_copy(zbuf, rows_ref.at[pl.ds(last, BLK_E)], zsem)

        def nonempty(e):
            return bend_ref[e] > (bend_ref[e - 1] if e else 0)

        for e in range(N_EXPERTS):
            pl.when(nonempty(e))(lambda e=e: tail_fill(e).start())
        for e in range(N_EXPERTS):
            pl.when(nonempty(e))(lambda e=e: tail_fill(e).wait())

        def unused_fill(b):
            return pltpu.make_async_copy(
                zbuf, rows_ref.at[pl.ds(pl.multiple_of(b * BLK_E, BLK_E), BLK_E)], zsem)

        used, total = bend_ref[N_EXPERTS - 1], rows_ref.shape[0] // BLK_E
        lax.fori_loop(used, total, lambda b, c: (unused_fill(b).start(), c)[1], 0)
        lax.fori_loop(used, total, lambda b, c: (unused_fill(b).wait(), c)[1], 0)

    base = i * TD_ROWS

    def start(t, carry):
        for k in range(TOP_K):
            _row_copy(x2p_hbm, base + t, rows_ref, dest_ref[t * TOP_K + k],
                      sem.at[i & 1]).start(priority=k % 2)
        return carry

    lax.fori_loop(0, TD_ROWS, start, 0, unroll=DMA_UNROLL)

    def wait_tile(parity):
        for k in range(TOP_K):
            pltpu.make_async_copy(x2p_hbm.at[pl.ds(0, TD_ROWS)], rows_ref.at[pl.ds(0, TD_ROWS)],
                                  sem.at[parity]).wait()

    pl.when(i > 0)(lambda: wait_tile(1 - (i & 1)))
    pl.when(i == last)(lambda: wait_tile(i & 1))


def _dispatch(bend, dest_flat, x2p, n_rows):
    N, W = x2p.shape
    smem_blk = pl.BlockSpec((TD_ROWS * TOP_K,), lambda i, be: (i,), memory_space=pltpu.SMEM)
    grid_spec = pltpu.PrefetchScalarGridSpec(
        num_scalar_prefetch=1,
        grid=(N // TD_ROWS,),
        in_specs=[smem_blk, pl.BlockSpec(memory_space=pl.ANY)],
        out_specs=pl.BlockSpec(memory_space=pl.ANY),
        scratch_shapes=[pltpu.VMEM((BLK_E, W), jnp.uint32), pltpu.SemaphoreType.DMA((2,)),
                        pltpu.SemaphoreType.DMA(())],
    )
    return pl.pallas_call(
        _dispatch_kernel,
        out_shape=jax.ShapeDtypeStruct((n_rows, W), jnp.uint32),
        grid_spec=grid_spec,
        compiler_params=pltpu.CompilerParams(
            dimension_semantics=("arbitrary",), has_side_effects=True),
        name="dispatch",
    )(bend, dest_flat, x2p)


def _experts_kernel(layer, be_ref, first_ref, nvalid_ref, next_ref, slot_ref,
                    rows_ref, bgu_ref, bd_ref, wgu_hbm, wd_hbm,
                    y_ref, wgu_f32, wd_f32, wgu_bf, wd_bf, sem):
    i = pl.program_id(0)

    def fetch(e, slot):
        return (pltpu.make_async_copy(wgu_hbm.at[layer, e], wgu_f32.at[slot], sem.at[0, slot]),
                pltpu.make_async_copy(wd_hbm.at[layer, e], wd_f32.at[slot], sem.at[1, slot]))

    @pl.when(i < nvalid_ref[0])
    def _():
        @pl.when(first_ref[i] == 1)
        def _():
            slot = slot_ref[i]

            @pl.when(i == 0)
            def _():
                for c in fetch(be_ref[i], slot):
                    c.start()

            for c in fetch(be_ref[i], slot):
                c.wait()
            wgu_bf[...] = wgu_f32[slot].astype(BF16)
            wd_bf[...] = wd_f32[slot].astype(BF16)

            @pl.when(next_ref[i] >= 0)
            def _():
                for c in fetch(next_ref[i], 1 - slot):
                    c.start()

        lo, hi = _unpack_bf16_pairs(rows_ref[...])
        half = D_MODEL // 2
        hgu = _dot(lo, wgu_bf[:half, :]) + _dot(hi, wgu_bf[half:, :]) + bgu_ref[...]
        g = jnp.minimum(hgu[:, :D_MODEL], SWIGLU_LIMIT)
        u = jnp.clip(hgu[:, D_MODEL:], -SWIGLU_LIMIT, SWIGLU_LIMIT)
        a = (u + 1.0) * (g * jax.nn.sigmoid(SWIGLU_ALPHA * g))
        y_ref[...] = _dot(a.astype(BF16), wd_bf[...]) + bd_ref[...]

    @pl.when(i >= nvalid_ref[0])
    def _():
        y_ref[...] = jnp.zeros_like(y_ref)


def _experts(layer, plan, rows, w_gu, b_gu, w_down, b_down):
    P, W = rows.shape
    L, E, D, F2 = w_gu.shape
    nb = P // BLK_E

    def row_map(i, be, bf, nv, nx, sl):
        return (jnp.minimum(i, nv[0] - 1), 0)

    def b_map(i, be, bf, nv, nx, sl):
        return (layer, be[i], 0, 0)

    grid_spec = pltpu.PrefetchScalarGridSpec(
        num_scalar_prefetch=5,
        grid=(nb,),
        in_specs=[pl.BlockSpec((BLK_E, W), row_map),
                  pl.BlockSpec((None, None, 1, F2), b_map),
                  pl.BlockSpec((None, None, 1, D), b_map),
                  pl.BlockSpec(memory_space=pl.ANY),
                  pl.BlockSpec(memory_space=pl.ANY)],
        out_specs=pl.BlockSpec((BLK_E, D), lambda i, *_: (i, 0)),
        scratch_shapes=[pltpu.VMEM((2, D, F2), F32), pltpu.VMEM((2, F2 // 2, D), F32),
                        pltpu.VMEM((D, F2), BF16), pltpu.VMEM((F2 // 2, D), BF16),
                        pltpu.SemaphoreType.DMA((2, 2))],
    )
    return pl.pallas_call(
        functools.partial(_experts_kernel, layer),
        out_shape=jax.ShapeDtypeStruct((P, D), F32),
        grid_spec=grid_spec,
        compiler_params=pltpu.CompilerParams(
            dimension_semantics=("arbitrary",), vmem_limit_bytes=VMEM_LIMIT),
        name="experts",
    )(plan.block_e, plan.block_first, plan.nvalid, plan.next_e, plan.slot, rows,
      b_gu.reshape(L, E, 1, F2), b_down.reshape(L, E, 1, D), w_gu, w_down)


def _combine_kernel(dest_ref, dest_next_ref, gate_ref, x2_ref, lng_ref, lnb_ref, y_ref, o_ref,
                    ybuf, sem):
    i = pl.program_id(0)
    slot = i & 1

    def gather(d_ref, buf):
        def start(t, carry):
            for k in range(TOP_K):
                _row_copy(y_ref, d_ref[t * TOP_K + k], ybuf.at[buf, k], t,
                          sem.at[buf]).start(priority=k % 2)
            return carry

        lax.fori_loop(0, TD_ROWS, start, 0, unroll=DMA_UNROLL)

    pl.when(i == 0)(lambda: gather(dest_ref, slot))
    pl.when(i + 1 < pl.num_programs(0))(lambda: gather(dest_next_ref, 1 - slot))
    for k in range(TOP_K):
        pltpu.make_async_copy(y_ref.at[pl.ds(0, TD_ROWS)], ybuf.at[slot, k], sem.at[slot]).wait()

    gates = gate_ref[...]
    ff = gates[:, 0:1] * ybuf[slot, 0]
    for k in range(1, TOP_K):
        ff = ff + gates[:, k:k + 1] * ybuf[slot, k]
    alpha = (2 * 2) ** 0.25
    o_ref[...] = _layer_norm(alpha * x2_ref[...] + ff, lng_ref[2:3, :], lnb_ref[2:3, :])


def _combine(dest_flat, gates, x2, ln_g, ln_b, y):
    N, D = x2.shape
    steps = N // TD_ROWS
    dest_blk = lambda ahead: pl.BlockSpec(
        (TD_ROWS * TOP_K,), lambda i: (jnp.minimum(i + ahead, steps - 1),), memory_space=pltpu.SMEM)
    return pl.pallas_call(
        _combine_kernel,
        out_shape=jax.ShapeDtypeStruct((N, D), F32),
        grid=(steps,),
        in_specs=[dest_blk(0), dest_blk(1),
                  pl.BlockSpec((TD_ROWS, TOP_K), lambda i: (i, 0)),
                  pl.BlockSpec((TD_ROWS, D), lambda i: (i, 0)),
                  pl.BlockSpec(ln_g.shape, lambda i: (0, 0)),
                  pl.BlockSpec(ln_b.shape, lambda i: (0, 0)),
                  pl.BlockSpec(memory_space=pl.ANY)],
        out_specs=pl.BlockSpec((TD_ROWS, D), lambda i: (i, 0)),
        scratch_shapes=[pltpu.VMEM((2, TOP_K, TD_ROWS, D), F32), pltpu.SemaphoreType.DMA((2,))],
        compiler_params=pltpu.CompilerParams(
            dimension_semantics=("arbitrary",), vmem_limit_bytes=VMEM_LIMIT),
        name="combine",
    )(dest_flat, dest_flat, gates, x2, ln_g, ln_b, y)


class _Plan(NamedTuple):
    pstart: jax.Array
    bend: jax.Array
    nvalid: jax.Array
    block_e: jax.Array
    block_first: jax.Array
    next_e: jax.Array
    slot: jax.Array


def _routing_plan(counts, nb):
    counts = counts.reshape(N_EXPERTS).astype(jnp.int32)
    nblk = (counts + BLK_E - 1) // BLK_E
    bend = jnp.cumsum(nblk)
    pstart = (bend - nblk) * BLK_E
    nvalid = bend[-1:]
    blocks = jnp.minimum(jnp.arange(nb, dtype=jnp.int32), nvalid[0] - 1)
    block_e = jnp.sum((blocks[:, None] >= bend[None, :]).astype(jnp.int32), axis=1)
    block_e = jnp.minimum(block_e, N_EXPERTS - 1)
    block_first = jnp.concatenate(
        [jnp.ones((1,), jnp.int32), (block_e[1:] != block_e[:-1]).astype(jnp.int32)])
    experts = jnp.arange(N_EXPERTS, dtype=jnp.int32)
    at_or_after = lax.cummin(jnp.where(nblk > 0, experts, N_EXPERTS), reverse=True)
    after = jnp.concatenate([at_or_after[1:], jnp.full((1,), N_EXPERTS, jnp.int32)])
    next_e = jnp.where(after < N_EXPERTS, after, -1)[block_e]
    slot = (jnp.cumsum(block_first) - 1) & 1
    return _Plan(pstart, bend, nvalid, block_e, block_first, next_e.astype(jnp.int32),
                 slot.astype(jnp.int32))


def _layer(layer, x, mem, w_in, sgu_g, sgu_b, w_sp, b_sp, grp_g, w_out, wq, wkv, wo,
           w_router, b_router, w_gu, b_gu, w_down, b_down, ln_g, ln_b):
    B, S, D = x.shape
    N = B * S
    row = lambda a: a.reshape(1, -1)

    ya, q, k, v = _mixer_in(x, w_in.astype(BF16), row(sgu_g), row(sgu_b), w_sp,
                            b_sp.reshape(SGU_GROUPS, CHUNK, 1), row(grp_g[:SGU_WIDTH]))
    yb = _sb_attn(q, k, v, row(grp_g[SGU_WIDTH:]))
    k_mem, v_mem = _kv_proj(mem, wkv.astype(BF16))
    x2, x2p, idx, rank, gates, counts = _post_mixer(
        x, ya, yb, w_out.astype(BF16), ln_g, ln_b, wq.astype(BF16), k_mem, v_mem,
        wo.astype(BF16), w_router, row(b_router))

    nb = (N * TOP_K) // BLK_E + N_EXPERTS
    plan = _routing_plan(counts, nb)
    dest = _dest_rows(plan.pstart, idx.reshape(-1), rank.reshape(-1))
    rows = _dispatch(plan.bend, dest, x2p.reshape(N, D // 2), nb * BLK_E)
    y = _experts(layer, plan, rows, w_gu, b_gu, w_down, b_down)
    out = _combine(dest, gates.reshape(N, TOP_K), x2.reshape(N, D), ln_g, ln_b, y)
    return out.reshape(B, S, D)


def kernel(x, mem, w_in, sgu_g, sgu_b, w_sp, b_sp, grp_g, w_out, wq_mem, wkv_mem, wo_mem,
           w_router, b_router, w_gu, b_gu, w_down, b_down, ln_g, ln_b):
    for l in range(w_in.shape[0]):
        x = _layer(l, x, mem, w_in[l], sgu_g[l], sgu_b[l], w_sp[l], b_sp[l], grp_g[l], w_out[l],
                   wq_mem[l], wkv_mem[l], wo_mem[l], w_router[l], b_router[l], w_gu, b_gu,
                   w_down, b_down, ln_g[l], ln_b[l])
    return x
```

```python
import functools
from typing import NamedTuple

import jax
import jax.numpy as jnp
from jax import lax
from jax.experimental import pallas as pl
from jax.experimental.pallas import tpu as pltpu

F32 = jnp.float32
BF16 = jnp.bfloat16

D_MODEL = 1024
SGU_WIDTH = 512
SGU_GROUPS = 4
GROUP_DIM = 128
CHUNK = 128
SB_WIDTH = 512
SB_HEADS = 8
SB_HEAD_DIM = 64
MEM_HEADS = 4
MEM_HEAD_DIM = 256
N_EXPERTS = 32
TOP_K = 4
SWIGLU_LIMIT = 7.0
SWIGLU_ALPHA = 1.702
LN_EPS = 1e-5
RMS_EPS = 1e-6
LOG2E = 1.4426950408889634

LANES = 128
TM_MIX = 1024
TQ_SB = 512
TM_POST = 1024
TD_ROWS = 256
BLK_E = 256
DMA_UNROLL = 4
VMEM_LIMIT = 56 * 1024 * 1024


def _layer_norm(x, g, b):
    mu = jnp.mean(x, axis=-1, keepdims=True)
    xc = x - mu
    var = jnp.mean(xc * xc, axis=-1, keepdims=True)
    return xc * lax.rsqrt(var + LN_EPS) * g + b


def _dot(a, b):
    return jnp.dot(a, b, preferred_element_type=F32)


def _dot_nt(a, b):
    return lax.dot_general(a, b, (((1,), (1,)), ((), ())), preferred_element_type=F32)


def _mixer_in_kernel(x_ref, win_ref, sg_ref, sb_ref, wsp_ref, bsp_ref, gg_ref,
                     ya_ref, q_ref, k_ref, v_ref):
    xb = x_ref[...].astype(BF16)

    def proj(lo, hi):
        return _dot(xb, win_ref[:, lo:hi])

    u = jax.nn.gelu(proj(0, SGU_WIDTH))
    vs = jax.nn.gelu(proj(SGU_WIDTH, 2 * SGU_WIDTH))
    r = lax.broadcasted_iota(jnp.int32, (CHUNK, CHUNK), 0)
    c = lax.broadcasted_iota(jnp.int32, (CHUNK, CHUNK), 1)
    causal = r >= c
    for g in range(SGU_GROUPS):
        gs = slice(g * GROUP_DIM, (g + 1) * GROUP_DIM)
        vn = _layer_norm(vs[:, gs], sg_ref[:, gs], sb_ref[:, gs]).astype(BF16)
        wc = jnp.where(causal, wsp_ref[g], 0.0).astype(BF16)
        for ch in range(TM_MIX // CHUNK):
            cs = slice(ch * CHUNK, (ch + 1) * CHUNK)
            gate = _dot(wc, vn[cs]) + bsp_ref[g]
            oa = u[cs, gs] * gate
            ms = jnp.mean(oa * oa, axis=-1, keepdims=True)
            ya_ref[cs, gs] = (oa * lax.rsqrt(ms + RMS_EPS) * gg_ref[:, gs]).astype(BF16)
    o = 2 * SGU_WIDTH
    q_ref[...] = (proj(o, o + SB_WIDTH) * (SB_HEAD_DIM ** -0.5)).astype(BF16)
    k_ref[...] = proj(o + SB_WIDTH, o + 2 * SB_WIDTH).astype(BF16)
    v_ref[...] = proj(o + 2 * SB_WIDTH, o + 3 * SB_WIDTH).astype(BF16)


def _mixer_in(x, w_in, sgu_g, sgu_b, w_sp, b_sp, grp_g_a):
    B, S, D = x.shape
    full = lambda shape: pl.BlockSpec(shape, lambda b, i: (0,) * len(shape))
    tile = lambda w: pl.BlockSpec((None, TM_MIX, w), lambda b, i: (b, i, 0))
    out = jax.ShapeDtypeStruct((B, S, SGU_WIDTH), BF16)
    return pl.pallas_call(
        _mixer_in_kernel,
        out_shape=(out, out, out, out),
        grid=(B, S // TM_MIX),
        in_specs=[tile(D), full(w_in.shape), full(sgu_g.shape), full(sgu_b.shape),
                  full(w_sp.shape), full(b_sp.shape), full(grp_g_a.shape)],
        out_specs=(tile(SGU_WIDTH),) * 4,
        compiler_params=pltpu.CompilerParams(
            dimension_semantics=("parallel", "parallel"), vmem_limit_bytes=VMEM_LIMIT),
        name="mixer_in",
    )(x, w_in, sgu_g, sgu_b, w_sp, b_sp, grp_g_a)


def _sb_attn_kernel(q_ref, k_ref, v_ref, gg_ref, o_ref):
    i = pl.program_id(2)
    lo = i * TQ_SB
    lane = lax.broadcasted_iota(jnp.int32, (1, LANES), 1)
    head0 = lane < SB_HEAD_DIM
    q_all = q_ref[...]
    r = lax.broadcasted_iota(jnp.int32, (2 * LANES, 2 * LANES), 0)
    c = lax.broadcasted_iota(jnp.int32, (2 * LANES, 2 * LANES), 1)
    after = jnp.where((r > c) & ((r // LANES) == (c // LANES)), 1.0, 0.0).astype(BF16)

    def split_heads(x):
        zx = jnp.zeros_like(x)
        return jnp.concatenate([jnp.where(head0, x, zx), jnp.where(head0, zx, x)], axis=0)

    earlier = ((lax.broadcasted_iota(jnp.int32, (LANES, 2 * LANES), 1) & (LANES - 1))
               < lax.broadcasted_iota(jnp.int32, (LANES, 2 * LANES), 0))

    def keep_earlier_keys(x):
        top = jnp.where(earlier, x[:LANES], 0.0)
        return jnp.concatenate([top, x[LANES:]], axis=0) if x.shape[0] > LANES else top

    def add_rows(x, r0, upd):
        new = x[r0:] + upd
        return jnp.concatenate([x[:r0], new], axis=0) if r0 else new

    def key_blocks(blocks, carry, acc):
        starts = [pl.multiple_of(s0, LANES) for _, s0, _ in blocks]
        k2s = [split_heads(k_ref[pl.ds(s, LANES), :]) for s in starts]
        v2s = [split_heads(v_ref[pl.ds(s, LANES), :]) for s in starts]
        zs = [_dot_nt(q_all[r0:], k2) for (r0, _, _), k2 in zip(blocks, k2s)]
        sps = []
        for (_, _, causal), z in zip(blocks, zs):
            sp = jnp.maximum(z, 0.0) + jnp.log(1.0 + jnp.exp2(jnp.abs(z) * (-LOG2E)))
            if causal:
                sp = keep_earlier_keys(sp)
            sps.append(sp)
        his = [sp.astype(BF16) for sp in sps]
        los = [(sp - hi.astype(F32)).astype(BF16) for sp, hi in zip(sps, his)]
        laters = [_dot(hi, after) + _dot(lo_part, after) for hi, lo_part in zip(his, los)]
        weights = []
        for (r0, _, causal), z, sp, later in zip(blocks, zs, sps, laters):
            a = jnp.exp(z - (sp + later + carry[r0:]))
            if causal:
                a = keep_earlier_keys(a)
            weights.append(a.astype(BF16))
            rows = [jnp.broadcast_to(jnp.sum(sp[:, h * LANES:(h + 1) * LANES], axis=1, keepdims=True),
                                     (z.shape[0], LANES)) for h in range(2)]
            carry = add_rows(carry, r0, jnp.concatenate(rows, axis=1))
        for (r0, _, _), a, v2 in zip(blocks, weights, v2s):
            acc = add_rows(acc, r0, _dot(a, v2))
        return carry, acc

    nsub = TQ_SB // LANES
    carry = jnp.zeros((TQ_SB, 2 * LANES), F32)
    acc = jnp.zeros((TQ_SB, LANES), F32)
    carry, acc = key_blocks([(sb * LANES, lo + sb * LANES, True) for sb in reversed(range(nsub))],
                            carry, acc)

    def body(j, st):
        s0 = pl.multiple_of(lo - (j + 1) * TQ_SB, TQ_SB)
        return key_blocks([(0, s0 + sb * LANES, False) for sb in reversed(range(nsub))], *st)

    carry, acc = lax.fori_loop(0, i, body, (carry, acc))

    sq = acc * acc
    ss0 = jnp.sum(jnp.where(head0, sq, 0.0), axis=1, keepdims=True)
    ss1 = jnp.sum(jnp.where(head0, 0.0, sq), axis=1, keepdims=True)
    inv = 1.0 / SB_HEAD_DIM
    rs = jnp.where(head0, lax.rsqrt(ss0 * inv + RMS_EPS), lax.rsqrt(ss1 * inv + RMS_EPS))
    o_ref[...] = (acc * rs * gg_ref[...]).astype(BF16)


def _sb_attn(q, k, v, grp_g_b):
    B, S, W = q.shape
    pairs = W // LANES
    return pl.pallas_call(
        _sb_attn_kernel,
        out_shape=jax.ShapeDtypeStruct((B, S, W), BF16),
        grid=(B, pairs, S // TQ_SB),
        in_specs=[pl.BlockSpec((None, TQ_SB, LANES), lambda b, p, i: (b, i, p)),
                  pl.BlockSpec((None, S, LANES), lambda b, p, i: (b, 0, p)),
                  pl.BlockSpec((None, S, LANES), lambda b, p, i: (b, 0, p)),
                  pl.BlockSpec((1, LANES), lambda b, p, i: (0, p))],
        out_specs=pl.BlockSpec((None, TQ_SB, LANES), lambda b, p, i: (b, i, p)),
        compiler_params=pltpu.CompilerParams(
            dimension_semantics=("parallel", "parallel", "arbitrary"),
            vmem_limit_bytes=VMEM_LIMIT),
        name="sb_attn",
    )(q, k, v, grp_g_b)


def _kv_proj_kernel(mem_ref, wkv_ref, k_ref, v_ref):
    mb = mem_ref[...].astype(BF16)
    k_ref[...] = _dot(mb, wkv_ref[:, :D_MODEL]).astype(BF16)
    v_ref[...] = _dot(mb, wkv_ref[:, D_MODEL:]).astype(BF16)


def _kv_proj(mem, wkv):
    B, M, D = mem.shape
    out = jax.ShapeDtypeStruct((B, M, D), BF16)
    return pl.pallas_call(
        _kv_proj_kernel,
        out_shape=(out, out),
        grid=(B,),
        in_specs=[pl.BlockSpec((None, M, D), lambda b: (b, 0, 0)),
                  pl.BlockSpec(wkv.shape, lambda b: (0, 0))],
        out_specs=(pl.BlockSpec((None, M, D), lambda b: (b, 0, 0)),) * 2,
        compiler_params=pltpu.CompilerParams(
            dimension_semantics=("parallel",), vmem_limit_bytes=VMEM_LIMIT),
        name="kv_proj",
    )(mem, wkv)


def _pack_bf16_pairs(x):
    w = x.shape[1] // 2
    bits = lax.bitcast_convert_type(x.astype(BF16).astype(F32), jnp.uint32)
    return (bits[:, w:] & jnp.uint32(0xFFFF0000)) | (bits[:, :w] >> 16)


def _unpack_bf16_pairs(u):
    lo = lax.bitcast_convert_type(u << 16, F32).astype(BF16)
    hi = lax.bitcast_convert_type(u & jnp.uint32(0xFFFF0000), F32).astype(BF16)
    return lo, hi


def _post_mixer_kernel(x_ref, ya_ref, yb_ref, wout_ref, lng_ref, lnb_ref, wq_ref, km_ref, vm_ref,
                       wo_ref, wr_ref, br_ref,
                       x2_ref, x2p_ref, idx_ref, rank_ref, gate_ref, cnt_ref, carry_ref):
    first = jnp.logical_and(pl.program_id(0) == 0, pl.program_id(1) == 0)

    @pl.when(first)
    def _():
        carry_ref[...] = jnp.zeros_like(carry_ref)

    alpha = (2 * 2) ** 0.25
    mix = _dot(ya_ref[...], wout_ref[:SGU_WIDTH, :]) + _dot(yb_ref[...], wout_ref[SGU_WIDTH:, :])
    x1 = _layer_norm(alpha * x_ref[...] + mix, lng_ref[0:1, :], lnb_ref[0:1, :])

    qm = (_dot(x1.astype(BF16), wq_ref[...]) * (MEM_HEAD_DIM ** -0.5)).astype(BF16)
    heads = []
    for h in range(MEM_HEADS):
        hs = slice(h * MEM_HEAD_DIM, (h + 1) * MEM_HEAD_DIM)
        s = _dot_nt(qm[:, hs], km_ref[:, hs])
        e = jnp.exp(s - jnp.max(s, axis=-1, keepdims=True))
        p = (e / jnp.sum(e, axis=-1, keepdims=True)).astype(BF16)
        heads.append(_dot(p, vm_ref[:, hs]).astype(BF16))
    o = jnp.concatenate(heads, axis=1)
    x2 = _layer_norm(alpha * x1 + _dot(o, wo_ref[...]), lng_ref[1:2, :], lnb_ref[1:2, :])
    x2_ref[...] = x2
    x2p_ref[...] = _pack_bf16_pairs(x2)

    xh = x2.astype(BF16)
    xl = (x2 - xh.astype(F32)).astype(BF16)
    wr = wr_ref[...]
    wh = wr.astype(BF16)
    wl = (wr - wh.astype(F32)).astype(BF16)
    logits = _dot(xh, wh) + (_dot(xl, wh) + _dot(xh, wl)) + br_ref[...]

    tm = logits.shape[0]
    lane = lax.broadcasted_iota(jnp.int32, (tm, N_EXPERTS), 1).astype(F32)
    work = logits
    vals, idxs = [], []
    chosen = jnp.zeros((tm, N_EXPERTS), F32)
    for _ in range(TOP_K):
        m = jnp.max(work, axis=-1, keepdims=True)
        ix = jnp.min(jnp.where(work == m, lane, float(N_EXPERTS)), axis=-1, keepdims=True)
        sel = lane == ix
        work = jnp.where(sel, -jnp.inf, work)
        chosen = jnp.where(sel, 1.0, chosen)
        vals.append(m)
        idxs.append(ix)
    ev = [jnp.exp(v - vals[0]) for v in vals]
    den = ev[0] + ev[1] + ev[2] + ev[3]

    r = lax.broadcasted_iota(jnp.int32, (tm, tm), 0)
    c = lax.broadcasted_iota(jnp.int32, (tm, tm), 1)
    before = jnp.where(r > c, 1.0, 0.0).astype(BF16)
    running = _dot(before, chosen.astype(BF16)) + carry_ref[...]
    ranks = [jnp.sum(jnp.where(lane == ix, running, 0.0), axis=-1, keepdims=True) for ix in idxs]
    carry_ref[...] = carry_ref[...] + jnp.sum(chosen, axis=0, keepdims=True)
    cnt_ref[...] = carry_ref[...]

    l4 = lax.broadcasted_iota(jnp.int32, (tm, TOP_K), 1)

    def cols(xs):
        return jnp.where(l4 == 0, xs[0], jnp.where(l4 == 1, xs[1], jnp.where(l4 == 2, xs[2], xs[3])))

    idx_ref[...] = cols(idxs).astype(jnp.int32)
    rank_ref[...] = cols(ranks).astype(jnp.int32)
    gate_ref[...] = cols([e / den for e in ev])


def _post_mixer(x, ya, yb, w_out, ln_g, ln_b, wq, k_mem, v_mem, wo, w_router, b_router):
    B, S, D = x.shape
    M = k_mem.shape[1]
    full = lambda a: pl.BlockSpec(a.shape, lambda b, i: (0,) * a.ndim)
    tile = lambda w: pl.BlockSpec((None, TM_POST, w), lambda b, i: (b, i, 0))
    memb = pl.BlockSpec((None, M, D), lambda b, i: (b, 0, 0))
    sds = jax.ShapeDtypeStruct
    return pl.pallas_call(
        _post_mixer_kernel,
        out_shape=(sds((B, S, D), F32), sds((B, S, D // 2), jnp.uint32),
                   sds((B, S, TOP_K), jnp.int32), sds((B, S, TOP_K), jnp.int32),
                   sds((B, S, TOP_K), F32), sds((1, N_EXPERTS), F32)),
        grid=(B, S // TM_POST),
        in_specs=[tile(D), tile(SGU_WIDTH), tile(SB_WIDTH), full(w_out), full(ln_g), full(ln_b),
                  full(wq), memb, memb, full(wo), full(w_router), full(b_router)],
        out_specs=(tile(D), tile(D // 2), tile(TOP_K), tile(TOP_K), tile(TOP_K),
                   pl.BlockSpec((1, N_EXPERTS), lambda b, i: (0, 0))),
        scratch_shapes=[pltpu.VMEM((1, N_EXPERTS), F32)],
        compiler_params=pltpu.CompilerParams(
            dimension_semantics=("arbitrary", "arbitrary"), vmem_limit_bytes=VMEM_LIMIT),
        name="post_mixer",
    )(x, ya, yb, w_out, ln_g, ln_b, wq, k_mem, v_mem, wo, w_router, b_router)


def _row_copy(src_ref, src_row, dst_ref, dst_row, sem):
    return pltpu.make_async_copy(src_ref.at[pl.ds(src_row, 1)], dst_ref.at[pl.ds(dst_row, 1)], sem)


def _dest_rows_kernel(pstart_ref, idx_ref, rank_ref, dest_ref):
    idx = idx_ref[...]
    base = jnp.zeros_like(idx)
    for e in range(N_EXPERTS):
        base = jnp.where(idx == e, pstart_ref[e], base)
    dest_ref[...] = base + rank_ref[...]


def _dest_rows(pstart, idx_flat, rank_flat):
    n = idx_flat.shape[0]
    shape2d = (n // LANES, LANES)
    full = pl.BlockSpec(shape2d, lambda i, ps: (0, 0))
    dest = pl.pallas_call(
        _dest_rows_kernel,
        out_shape=jax.ShapeDtypeStruct(shape2d, jnp.int32),
        grid_spec=pltpu.PrefetchScalarGridSpec(
            num_scalar_prefetch=1, grid=(1,), in_specs=[full, full], out_specs=full),
        name="dest_rows",
    )(pstart, idx_flat.reshape(shape2d), rank_flat.reshape(shape2d))
    return dest.reshape(n)


def _dispatch_kernel(bend_ref, dest_ref, x2p_ref, rows_ref, zbuf, sem, zsem):
    @pl.when(pl.program_id(0) == 0)
    def _():
        zbuf[...] = jnp.zeros_like(zbuf)

        def tail_fill(e):
            last = pl.multiple_of((bend_ref[e] - 1) * BLK_E, BLK_E)
            return pltpu.make_async_copy(zbuf, rows_ref.at[pl.ds(last, BLK_E)], zsem)

        def nonempty(e):
            return bend_ref[e] > (bend_ref[e - 1] if e else 0)

        for e in range(N_EXPERTS):
            pl.when(nonempty(e))(lambda e=e: tail_fill(e).start())
        for e in range(N_EXPERTS):
            pl.when(nonempty(e))(lambda e=e: tail_fill(e).wait())

        def unused_fill(b):
            return pltpu.make_async_copy(
                zbuf, rows_ref.at[pl.ds(pl.multiple_of(b * BLK_E, BLK_E), BLK_E)], zsem)

        used, total = bend_ref[N_EXPERTS - 1], rows_ref.shape[0] // BLK_E
        lax.fori_loop(used, total, lambda b, c: (unused_fill(b).start(), c)[1], 0)
        lax.fori_loop(used, total, lambda b, c: (unused_fill(b).wait(), c)[1], 0)

    def start(t, carry):
        for k in range(TOP_K):
            _row_copy(x2p_ref, t, rows_ref, dest_ref[t * TOP_K + k], sem).start(priority=k % 2)
        return carry

    lax.fori_loop(0, TD_ROWS, start, 0, unroll=DMA_UNROLL)
    for k in range(TOP_K):
        pltpu.make_async_copy(x2p_ref, rows_ref.at[pl.ds(0, TD_ROWS)], sem).wait()


def _dispatch(bend, dest_flat, x2p, n_rows):
    N, W = x2p.shape
    smem_blk = pl.BlockSpec((TD_ROWS * TOP_K,), lambda i, be: (i,), memory_space=pltpu.SMEM)
    grid_spec = pltpu.PrefetchScalarGridSpec(
        num_scalar_prefetch=1,
        grid=(N // TD_ROWS,),
        in_specs=[smem_blk, pl.BlockSpec((TD_ROWS, W), lambda i, be: (i, 0))],
        out_specs=pl.BlockSpec(memory_space=pl.ANY),
        scratch_shapes=[pltpu.VMEM((BLK_E, W), jnp.uint32), pltpu.SemaphoreType.DMA(()),
                        pltpu.SemaphoreType.DMA(())],
    )
    return pl.pallas_call(
        _dispatch_kernel,
        out_shape=jax.ShapeDtypeStruct((n_rows, W), jnp.uint32),
        grid_spec=grid_spec,
        compiler_params=pltpu.CompilerParams(
            dimension_semantics=("arbitrary",), has_side_effects=True),
        name="dispatch",
    )(bend, dest_flat, x2p)


def _experts_kernel(layer, be_ref, first_ref, nvalid_ref, next_ref, slot_ref,
                    rows_ref, bgu_ref, bd_ref, wgu_hbm, wd_hbm,
                    y_ref, wgu_f32, wd_f32, wgu_bf, wd_bf, sem):
    i = pl.program_id(0)

    def fetch(e, slot):
        return (pltpu.make_async_copy(wgu_hbm.at[layer, e], wgu_f32.at[slot], sem.at[0, slot]),
                pltpu.make_async_copy(wd_hbm.at[layer, e], wd_f32.at[slot], sem.at[1, slot]))

    @pl.when(i < nvalid_ref[0])
    def _():
        @pl.when(first_ref[i] == 1)
        def _():
            slot = slot_ref[i]

            @pl.when(i == 0)
            def _():
                for c in fetch(be_ref[i], slot):
                    c.start()

            for c in fetch(be_ref[i], slot):
                c.wait()
            wgu_bf[...] = wgu_f32[slot].astype(BF16)
            wd_bf[...] = wd_f32[slot].astype(BF16)

            @pl.when(next_ref[i] >= 0)
            def _():
                for c in fetch(next_ref[i], 1 - slot):
                    c.start()

        lo, hi = _unpack_bf16_pairs(rows_ref[...])
        half = D_MODEL // 2
        hgu = _dot(lo, wgu_bf[:half, :]) + _dot(hi, wgu_bf[half:, :]) + bgu_ref[...]
        g = jnp.minimum(hgu[:, :D_MODEL], SWIGLU_LIMIT)
        u = jnp.clip(hgu[:, D_MODEL:], -SWIGLU_LIMIT, SWIGLU_LIMIT)
        a = (u + 1.0) * (g * jax.nn.sigmoid(SWIGLU_ALPHA * g))
        y_ref[...] = _dot(a.astype(BF16), wd_bf[...]) + bd_ref[...]

    @pl.when(i >= nvalid_ref[0])
    def _():
        y_ref[...] = jnp.zeros_like(y_ref)


def _experts(layer, plan, rows, w_gu, b_gu, w_down, b_down):
    P, W = rows.shape
    L, E, D, F2 = w_gu.shape
    nb = P // BLK_E

    def row_map(i, be, bf, nv, nx, sl):
        return (jnp.minimum(i, nv[0] - 1), 0)

    def b_map(i, be, bf, nv, nx, sl):
        return (layer, be[i], 0, 0)

    grid_spec = pltpu.PrefetchScalarGridSpec(
        num_scalar_prefetch=5,
        grid=(nb,),
        in_specs=[pl.BlockSpec((BLK_E, W), row_map),
                  pl.BlockSpec((None, None, 1, F2), b_map),
                  pl.BlockSpec((None, None, 1, D), b_map),
                  pl.BlockSpec(memory_space=pl.ANY),
                  pl.BlockSpec(memory_space=pl.ANY)],
        out_specs=pl.BlockSpec((BLK_E, D), lambda i, *_: (i, 0)),
        scratch_shapes=[pltpu.VMEM((2, D, F2), F32), pltpu.VMEM((2, F2 // 2, D), F32),
                        pltpu.VMEM((D, F2), BF16), pltpu.VMEM((F2 // 2, D), BF16),
                        pltpu.SemaphoreType.DMA((2, 2))],
    )
    return pl.pallas_call(
        functools.partial(_experts_kernel, layer),
        out_shape=jax.ShapeDtypeStruct((P, D), F32),
        grid_spec=grid_spec,
        compiler_params=pltpu.CompilerParams(
            dimension_semantics=("arbitrary",), vmem_limit_bytes=VMEM_LIMIT),
        name="experts",
    )(plan.block_e, plan.block_first, plan.nvalid, plan.next_e, plan.slot, rows,
      b_gu.reshape(L, E, 1, F2), b_down.reshape(L, E, 1, D), w_gu, w_down)


def _combine_kernel(dest_ref, dest_next_ref, gate_ref, x2_ref, lng_ref, lnb_ref, y_ref, o_ref,
                    ybuf, sem):
    i = pl.program_id(0)
    slot = i & 1

    def gather(d_ref, buf):
        def start(t, carry):
            for k in range(TOP_K):
                _row_copy(y_ref, d_ref[t * TOP_K + k], ybuf.at[buf, k], t,
                          sem.at[buf]).start(priority=k % 2)
            return carry

        lax.fori_loop(0, TD_ROWS, start, 0, unroll=DMA_UNROLL)

    pl.when(i == 0)(lambda: gather(dest_ref, slot))
    pl.when(i + 1 < pl.num_programs(0))(lambda: gather(dest_next_ref, 1 - slot))
    for k in range(TOP_K):
        pltpu.make_async_copy(y_ref.at[pl.ds(0, TD_ROWS)], ybuf.at[slot, k], sem.at[slot]).wait()

    gates = gate_ref[...]
    ff = gates[:, 0:1] * ybuf[slot, 0]
    for k in range(1, TOP_K):
        ff = ff + gates[:, k:k + 1] * ybuf[slot, k]
    alpha = (2 * 2) ** 0.25
    o_ref[...] = _layer_norm(alpha * x2_ref[...] + ff, lng_ref[2:3, :], lnb_ref[2:3, :])


def _combine(dest_flat, gates, x2, ln_g, ln_b, y):
    N, D = x2.shape
    steps = N // TD_ROWS
    dest_blk = lambda ahead: pl.BlockSpec(
        (TD_ROWS * TOP_K,), lambda i: (jnp.minimum(i + ahead, steps - 1),), memory_space=pltpu.SMEM)
    return pl.pallas_call(
        _combine_kernel,
        out_shape=jax.ShapeDtypeStruct((N, D), F32),
        grid=(steps,),
        in_specs=[dest_blk(0), dest_blk(1),
                  pl.BlockSpec((TD_ROWS, TOP_K), lambda i: (i, 0)),
                  pl.BlockSpec((TD_ROWS, D), lambda i: (i, 0)),
                  pl.BlockSpec(ln_g.shape, lambda i: (0, 0)),
                  pl.BlockSpec(ln_b.shape, lambda i: (0, 0)),
                  pl.BlockSpec(memory_space=pl.ANY)],
        out_specs=pl.BlockSpec((TD_ROWS, D), lambda i: (i, 0)),
        scratch_shapes=[pltpu.VMEM((2, TOP_K, TD_ROWS, D), F32), pltpu.SemaphoreType.DMA((2,))],
        compiler_params=pltpu.CompilerParams(
            dimension_semantics=("arbitrary",), vmem_limit_bytes=VMEM_LIMIT),
        name="combine",
    )(dest_flat, dest_flat, gates, x2, ln_g, ln_b, y)


class _Plan(NamedTuple):
    pstart: jax.Array
    bend: jax.Array
    nvalid: jax.Array
    block_e: jax.Array
    block_first: jax.Array
    next_e: jax.Array
    slot: jax.Array


def _routing_plan(counts, nb):
    counts = counts.reshape(N_EXPERTS).astype(jnp.int32)
    nblk = (counts + BLK_E - 1) // BLK_E
    bend = jnp.cumsum(nblk)
    pstart = (bend - nblk) * BLK_E
    nvalid = bend[-1:]
    blocks = jnp.minimum(jnp.arange(nb, dtype=jnp.int32), nvalid[0] - 1)
    block_e = jnp.sum((blocks[:, None] >= bend[None, :]).astype(jnp.int32), axis=1)
    block_e = jnp.minimum(block_e, N_EXPERTS - 1)
    block_first = jnp.concatenate(
        [jnp.ones((1,), jnp.int32), (block_e[1:] != block_e[:-1]).astype(jnp.int32)])
    experts = jnp.arange(N_EXPERTS, dtype=jnp.int32)
    at_or_after = lax.cummin(jnp.where(nblk > 0, experts, N_EXPERTS), reverse=True)
    after = jnp.concatenate([at_or_after[1:], jnp.full((1,), N_EXPERTS, jnp.int32)])
    next_e = jnp.where(after < N_EXPERTS, after, -1)[block_e]
    slot = (jnp.cumsum(block_first) - 1) & 1
    return _Plan(pstart, bend, nvalid, block_e, block_first, next_e.astype(jnp.int32),
                 slot.astype(jnp.int32))


def _layer(layer, x, mem, w_in, sgu_g, sgu_b, w_sp, b_sp, grp_g, w_out, wq, wkv, wo,
           w_router, b_router, w_gu, b_gu, w_down, b_down, ln_g, ln_b):
    B, S, D = x.shape
    N = B * S
    row = lambda a: a.reshape(1, -1)

    ya, q, k, v = _mixer_in(x, w_in.astype(BF16), row(sgu_g), row(sgu_b), w_sp,
                            b_sp.reshape(SGU_GROUPS, CHUNK, 1), row(grp_g[:SGU_WIDTH]))
    yb = _sb_attn(q, k, v, row(grp_g[SGU_WIDTH:]))
    k_mem, v_mem = _kv_proj(mem, wkv.astype(BF16))
    x2, x2p, idx, rank, gates, counts = _post_mixer(
        x, ya, yb, w_out.astype(BF16), ln_g, ln_b, wq.astype(BF16), k_mem, v_mem,
        wo.astype(BF16), w_router, row(b_router))

    nb = (N * TOP_K) // BLK_E + N_EXPERTS
    plan = _routing_plan(counts, nb)
    dest = _dest_rows(plan.pstart, idx.reshape(-1), rank.reshape(-1))
    rows = _dispatch(plan.bend, dest, x2p.reshape(N, D // 2), nb * BLK_E)
    y = _experts(layer, plan, rows, w_gu, b_gu, w_down, b_down)
    out = _combine(dest, gates.reshape(N, TOP_K), x2.reshape(N, D), ln_g, ln_b, y)
    return out.reshape(B, S, D)


def kernel(x, mem, w_in, sgu_g, sgu_b, w_sp, b_sp, grp_g, w_out, wq_mem, wkv_mem, wo_mem,
           w_router, b_router, w_gu, b_gu, w_down, b_down, ln_g, ln_b):
    for l in range(w_in.shape[0]):
        x = _layer(l, x, mem, w_in[l], sgu_g[l], sgu_b[l], w_sp[l], b_sp[l], grp_g[l], w_out[l],
                   wq_mem[l], wkv_mem[l], wo_mem[l], w_router[l], b_router[l], w_gu, b_gu,
                   w_down, b_down, ln_g[l], ln_b[l])
    return x
```

```python
import functools
from typing import NamedTuple

import jax
import jax.numpy as jnp
from jax import lax
from jax.experimental import pallas as pl
from jax.experimental.pallas import tpu as pltpu
from jax.experimental.pallas import tpu_sc as plsc

F32 = jnp.float32
BF16 = jnp.bfloat16

D_MODEL = 1024
SGU_WIDTH = 512
SGU_GROUPS = 4
GROUP_DIM = 128
CHUNK = 128
SB_WIDTH = 512
SB_HEADS = 8
SB_HEAD_DIM = 64
MEM_HEADS = 4
MEM_HEAD_DIM = 256
N_EXPERTS = 32
TOP_K = 4
SWIGLU_LIMIT = 7.0
SWIGLU_ALPHA = 1.702
LN_EPS = 1e-5
RMS_EPS = 1e-6
LOG2E = 1.4426950408889634

LANES = 128
TM_MIX = 1024
TQ_SB = 512
TM_POST = 1024
TD_ROWS = 256
BLK_E = 256
DMA_UNROLL = 4
SC_CORES = 2
SC_SUBCORES = 16
SC_CHUNK = 64
VMEM_LIMIT = 56 * 1024 * 1024


def _layer_norm(x, g, b):
    mu = jnp.mean(x, axis=-1, keepdims=True)
    xc = x - mu
    var = jnp.mean(xc * xc, axis=-1, keepdims=True)
    return xc * lax.rsqrt(var + LN_EPS) * g + b


def _dot(a, b):
    return jnp.dot(a, b, preferred_element_type=F32)


def _dot_nt(a, b):
    return lax.dot_general(a, b, (((1,), (1,)), ((), ())), preferred_element_type=F32)


def _mixer_in_kernel(x_ref, win_ref, sg_ref, sb_ref, wsp_ref, bsp_ref, gg_ref,
                     ya_ref, q_ref, k_ref, v_ref):
    xb = x_ref[...].astype(BF16)

    def proj(lo, hi):
        return _dot(xb, win_ref[:, lo:hi])

    u = jax.nn.gelu(proj(0, SGU_WIDTH))
    vs = jax.nn.gelu(proj(SGU_WIDTH, 2 * SGU_WIDTH))
    r = lax.broadcasted_iota(jnp.int32, (CHUNK, CHUNK), 0)
    c = lax.broadcasted_iota(jnp.int32, (CHUNK, CHUNK), 1)
    causal = r >= c
    for g in range(SGU_GROUPS):
        gs = slice(g * GROUP_DIM, (g + 1) * GROUP_DIM)
        vn = _layer_norm(vs[:, gs], sg_ref[:, gs], sb_ref[:, gs]).astype(BF16)
        wc = jnp.where(causal, wsp_ref[g], 0.0).astype(BF16)
        for ch in range(TM_MIX // CHUNK):
            cs = slice(ch * CHUNK, (ch + 1) * CHUNK)
            gate = _dot(wc, vn[cs]) + bsp_ref[g]
            oa = u[cs, gs] * gate
            ms = jnp.mean(oa * oa, axis=-1, keepdims=True)
            ya_ref[cs, gs] = (oa * lax.rsqrt(ms + RMS_EPS) * gg_ref[:, gs]).astype(BF16)
    o = 2 * SGU_WIDTH
    q_ref[...] = (proj(o, o + SB_WIDTH) * (SB_HEAD_DIM ** -0.5)).astype(BF16)
    k_ref[...] = proj(o + SB_WIDTH, o + 2 * SB_WIDTH).astype(BF16)
    v_ref[...] = proj(o + 2 * SB_WIDTH, o + 3 * SB_WIDTH).astype(BF16)


def _mixer_in(x, w_in, sgu_g, sgu_b, w_sp, b_sp, grp_g_a):
    B, S, D = x.shape
    full = lambda shape: pl.BlockSpec(shape, lambda b, i: (0,) * len(shape))
    tile = lambda w: pl.BlockSpec((None, TM_MIX, w), lambda b, i: (b, i, 0))
    out = jax.ShapeDtypeStruct((B, S, SGU_WIDTH), BF16)
    return pl.pallas_call(
        _mixer_in_kernel,
        out_shape=(out, out, out, out),
        grid=(B, S // TM_MIX),
        in_specs=[tile(D), full(w_in.shape), full(sgu_g.shape), full(sgu_b.shape),
                  full(w_sp.shape), full(b_sp.shape), full(grp_g_a.shape)],
        out_specs=(tile(SGU_WIDTH),) * 4,
        compiler_params=pltpu.CompilerParams(
            dimension_semantics=("parallel", "parallel"), vmem_limit_bytes=VMEM_LIMIT),
        name="mixer_in",
    )(x, w_in, sgu_g, sgu_b, w_sp, b_sp, grp_g_a)


def _sb_attn_kernel(q_ref, k_ref, v_ref, gg_ref, o_ref):
    i = pl.program_id(2)
    lo = i * TQ_SB
    lane = lax.broadcasted_iota(jnp.int32, (1, LANES), 1)
    head0 = lane < SB_HEAD_DIM
    q_all = q_ref[...]
    r = lax.broadcasted_iota(jnp.int32, (2 * LANES, 2 * LANES), 0)
    c = lax.broadcasted_iota(jnp.int32, (2 * LANES, 2 * LANES), 1)
    after = jnp.where((r > c) & ((r // LANES) == (c // LANES)), 1.0, 0.0).astype(BF16)

    def split_heads(x):
        zx = jnp.zeros_like(x)
        return jnp.concatenate([jnp.where(head0, x, zx), jnp.where(head0, zx, x)], axis=0)

    earlier = ((lax.broadcasted_iota(jnp.int32, (LANES, 2 * LANES), 1) & (LANES - 1))
               < lax.broadcasted_iota(jnp.int32, (LANES, 2 * LANES), 0))

    def keep_earlier_keys(x):
        top = jnp.where(earlier, x[:LANES], 0.0)
        return jnp.concatenate([top, x[LANES:]], axis=0) if x.shape[0] > LANES else top

    def add_rows(x, r0, upd):
        new = x[r0:] + upd
        return jnp.concatenate([x[:r0], new], axis=0) if r0 else new

    def key_blocks(blocks, carry, acc):
        starts = [pl.multiple_of(s0, LANES) for _, s0, _ in blocks]
        k2s = [split_heads(k_ref[pl.ds(s, LANES), :]) for s in starts]
        v2s = [split_heads(v_ref[pl.ds(s, LANES), :]) for s in starts]
        zs = [_dot_nt(q_all[r0:], k2) for (r0, _, _), k2 in zip(blocks, k2s)]
        sps = []
        for (_, _, causal), z in zip(blocks, zs):
            sp = jnp.maximum(z, 0.0) + jnp.log(1.0 + jnp.exp2(jnp.abs(z) * (-LOG2E)))
            if causal:
                sp = keep_earlier_keys(sp)
            sps.append(sp)
        his = [sp.astype(BF16) for sp in sps]
        los = [(sp - hi.astype(F32)).astype(BF16) for sp, hi in zip(sps, his)]
        laters = [_dot(hi, after) + _dot(lo_part, after) for hi, lo_part in zip(his, los)]
        weights = []
        for (r0, _, causal), z, sp, later in zip(blocks, zs, sps, laters):
            a = jnp.exp(z - (sp + later + carry[r0:]))
            if causal:
                a = keep_earlier_keys(a)
            weights.append(a.astype(BF16))
            rows = [jnp.broadcast_to(jnp.sum(sp[:, h * LANES:(h + 1) * LANES], axis=1, keepdims=True),
                                     (z.shape[0], LANES)) for h in range(2)]
            carry = add_rows(carry, r0, jnp.concatenate(rows, axis=1))
        for (r0, _, _), a, v2 in zip(blocks, weights, v2s):
            acc = add_rows(acc, r0, _dot(a, v2))
        return carry, acc

    nsub = TQ_SB // LANES
    carry = jnp.zeros((TQ_SB, 2 * LANES), F32)
    acc = jnp.zeros((TQ_SB, LANES), F32)
    carry, acc = key_blocks([(sb * LANES, lo + sb * LANES, True) for sb in reversed(range(nsub))],
                            carry, acc)

    def body(j, st):
        s0 = pl.multiple_of(lo - (j + 1) * TQ_SB, TQ_SB)
        return key_blocks([(0, s0 + sb * LANES, False) for sb in reversed(range(nsub))], *st)

    carry, acc = lax.fori_loop(0, i, body, (carry, acc))

    sq = acc * acc
    ss0 = jnp.sum(jnp.where(head0, sq, 0.0), axis=1, keepdims=True)
    ss1 = jnp.sum(jnp.where(head0, 0.0, sq), axis=1, keepdims=True)
    inv = 1.0 / SB_HEAD_DIM
    rs = jnp.where(head0, lax.rsqrt(ss0 * inv + RMS_EPS), lax.rsqrt(ss1 * inv + RMS_EPS))
    o_ref[...] = (acc * rs * gg_ref[...]).astype(BF16)


def _sb_attn(q, k, v, grp_g_b):
    B, S, W = q.shape
    pairs = W // LANES
    return pl.pallas_call(
        _sb_attn_kernel,
        out_shape=jax.ShapeDtypeStruct((B, S, W), BF16),
        grid=(B, pairs, S // TQ_SB),
        in_specs=[pl.BlockSpec((None, TQ_SB, LANES), lambda b, p, i: (b, i, p)),
                  pl.BlockSpec((None, S, LANES), lambda b, p, i: (b, 0, p)),
                  pl.BlockSpec((None, S, LANES), lambda b, p, i: (b, 0, p)),
                  pl.BlockSpec((1, LANES), lambda b, p, i: (0, p))],
        out_specs=pl.BlockSpec((None, TQ_SB, LANES), lambda b, p, i: (b, i, p)),
        compiler_params=pltpu.CompilerParams(
            dimension_semantics=("parallel", "parallel", "arbitrary"),
            vmem_limit_bytes=VMEM_LIMIT),
        name="sb_attn",
    )(q, k, v, grp_g_b)


def _kv_proj_kernel(mem_ref, wkv_ref, k_ref, v_ref):
    mb = mem_ref[...].astype(BF16)
    k_ref[...] = _dot(mb, wkv_ref[:, :D_MODEL]).astype(BF16)
    v_ref[...] = _dot(mb, wkv_ref[:, D_MODEL:]).astype(BF16)


def _kv_proj(mem, wkv):
    B, M, D = mem.shape
    out = jax.ShapeDtypeStruct((B, M, D), BF16)
    return pl.pallas_call(
        _kv_proj_kernel,
        out_shape=(out, out),
        grid=(B,),
        in_specs=[pl.BlockSpec((None, M, D), lambda b: (b, 0, 0)),
                  pl.BlockSpec(wkv.shape, lambda b: (0, 0))],
        out_specs=(pl.BlockSpec((None, M, D), lambda b: (b, 0, 0)),) * 2,
        compiler_params=pltpu.CompilerParams(
            dimension_semantics=("parallel",), vmem_limit_bytes=VMEM_LIMIT),
        name="kv_proj",
    )(mem, wkv)


def _pack_bf16_pairs(x):
    w = x.shape[1] // 2
    bits = lax.bitcast_convert_type(x.astype(BF16).astype(F32), jnp.uint32)
    return (bits[:, w:] & jnp.uint32(0xFFFF0000)) | (bits[:, :w] >> 16)


def _unpack_bf16_pairs(u):
    lo = lax.bitcast_convert_type(u << 16, F32).astype(BF16)
    hi = lax.bitcast_convert_type(u & jnp.uint32(0xFFFF0000), F32).astype(BF16)
    return lo, hi


def _post_mixer_kernel(x_ref, ya_ref, yb_ref, wout_ref, lng_ref, lnb_ref, wq_ref, km_ref, vm_ref,
                       wo_ref, wr_ref, br_ref,
                       x2_ref, x2p_ref, idx_ref, rank_ref, gate_ref, cnt_ref, carry_ref):
    first = jnp.logical_and(pl.program_id(0) == 0, pl.program_id(1) == 0)

    @pl.when(first)
    def _():
        carry_ref[...] = jnp.zeros_like(carry_ref)

    alpha = (2 * 2) ** 0.25
    mix = _dot(ya_ref[...], wout_ref[:SGU_WIDTH, :]) + _dot(yb_ref[...], wout_ref[SGU_WIDTH:, :])
    x1 = _layer_norm(alpha * x_ref[...] + mix, lng_ref[0:1, :], lnb_ref[0:1, :])

    qm = (_dot(x1.astype(BF16), wq_ref[...]) * (MEM_HEAD_DIM ** -0.5)).astype(BF16)
    heads = []
    for h in range(MEM_HEADS):
        hs = slice(h * MEM_HEAD_DIM, (h + 1) * MEM_HEAD_DIM)
        s = _dot_nt(qm[:, hs], km_ref[:, hs])
        e = jnp.exp(s - jnp.max(s, axis=-1, keepdims=True))
        p = (e / jnp.sum(e, axis=-1, keepdims=True)).astype(BF16)
        heads.append(_dot(p, vm_ref[:, hs]).astype(BF16))
    o = jnp.concatenate(heads, axis=1)
    x2 = _layer_norm(alpha * x1 + _dot(o, wo_ref[...]), lng_ref[1:2, :], lnb_ref[1:2, :])
    x2_ref[...] = x2
    x2p_ref[...] = _pack_bf16_pairs(x2)

    xh = x2.astype(BF16)
    xl = (x2 - xh.astype(F32)).astype(BF16)
    wr = wr_ref[...]
    wh = wr.astype(BF16)
    wl = (wr - wh.astype(F32)).astype(BF16)
    logits = _dot(xh, wh) + (_dot(xl, wh) + _dot(xh, wl)) + br_ref[...]

    tm = logits.shape[0]
    lane = lax.broadcasted_iota(jnp.int32, (tm, N_EXPERTS), 1).astype(F32)
    work = logits
    vals, idxs = [], []
    chosen = jnp.zeros((tm, N_EXPERTS), F32)
    for _ in range(TOP_K):
        m = jnp.max(work, axis=-1, keepdims=True)
        ix = jnp.min(jnp.where(work == m, lane, float(N_EXPERTS)), axis=-1, keepdims=True)
        sel = lane == ix
        work = jnp.where(sel, -jnp.inf, work)
        chosen = jnp.where(sel, 1.0, chosen)
        vals.append(m)
        idxs.append(ix)
    ev = [jnp.exp(v - vals[0]) for v in vals]
    den = ev[0] + ev[1] + ev[2] + ev[3]

    r = lax.broadcasted_iota(jnp.int32, (tm, tm), 0)
    c = lax.broadcasted_iota(jnp.int32, (tm, tm), 1)
    before = jnp.where(r > c, 1.0, 0.0).astype(BF16)
    running = _dot(before, chosen.astype(BF16)) + carry_ref[...]
    ranks = [jnp.sum(jnp.where(lane == ix, running, 0.0), axis=-1, keepdims=True) for ix in idxs]
    carry_ref[...] = carry_ref[...] + jnp.sum(chosen, axis=0, keepdims=True)
    cnt_ref[...] = carry_ref[...]

    l4 = lax.broadcasted_iota(jnp.int32, (tm, TOP_K), 1)

    def cols(xs):
        return jnp.where(l4 == 0, xs[0], jnp.where(l4 == 1, xs[1], jnp.where(l4 == 2, xs[2], xs[3])))

    idx_ref[...] = cols(idxs).astype(jnp.int32)
    rank_ref[...] = cols(ranks).astype(jnp.int32)
    gate_ref[...] = cols([e / den for e in ev])


def _post_mixer(x, ya, yb, w_out, ln_g, ln_b, wq, k_mem, v_mem, wo, w_router, b_router):
    B, S, D = x.shape
    M = k_mem.shape[1]
    full = lambda a: pl.BlockSpec(a.shape, lambda b, i: (0,) * a.ndim)
    tile = lambda w: pl.BlockSpec((None, TM_POST, w), lambda b, i: (b, i, 0))
    memb = pl.BlockSpec((None, M, D), lambda b, i: (b, 0, 0))
    sds = jax.ShapeDtypeStruct
    return pl.pallas_call(
        _post_mixer_kernel,
        out_shape=(sds((B, S, D), F32), sds((B, S, D // 2), jnp.uint32),
                   sds((B, S, TOP_K), jnp.int32), sds((B, S, TOP_K), jnp.int32),
                   sds((B, S, TOP_K), F32), sds((1, N_EXPERTS), F32)),
        grid=(B, S // TM_POST),
        in_specs=[tile(D), tile(SGU_WIDTH), tile(SB_WIDTH), full(w_out), full(ln_g), full(ln_b),
                  full(wq), memb, memb, full(wo), full(w_router), full(b_router)],
        out_specs=(tile(D), tile(D // 2), tile(TOP_K), tile(TOP_K), tile(TOP_K),
                   pl.BlockSpec((1, N_EXPERTS), lambda b, i: (0, 0))),
        scratch_shapes=[pltpu.VMEM((1, N_EXPERTS), F32)],
        compiler_params=pltpu.CompilerParams(
            dimension_semantics=("arbitrary", "arbitrary"), vmem_limit_bytes=VMEM_LIMIT),
        name="post_mixer",
    )(x, ya, yb, w_out, ln_g, ln_b, wq, k_mem, v_mem, wo, w_router, b_router)


def _row_copy(src_ref, src_row, dst_ref, dst_row, sem):
    return pltpu.make_async_copy(src_ref.at[pl.ds(src_row, 1)], dst_ref.at[pl.ds(dst_row, 1)], sem)


def _dest_rows_kernel(pstart_ref, idx_ref, rank_ref, dest_ref):
    idx = idx_ref[...]
    base = jnp.zeros_like(idx)
    for e in range(N_EXPERTS):
        base = jnp.where(idx == e, pstart_ref[e], base)
    dest_ref[...] = base + rank_ref[...]


def _dest_rows(pstart, idx_flat, rank_flat):
    n = idx_flat.shape[0]
    shape2d = (n // LANES, LANES)
    full = pl.BlockSpec(shape2d, lambda i, ps: (0, 0))
    dest = pl.pallas_call(
        _dest_rows_kernel,
        out_shape=jax.ShapeDtypeStruct(shape2d, jnp.int32),
        grid_spec=pltpu.PrefetchScalarGridSpec(
            num_scalar_prefetch=1, grid=(1,), in_specs=[full, full], out_specs=full),
        name="dest_rows",
    )(pstart, idx_flat.reshape(shape2d), rank_flat.reshape(shape2d))
    return dest.reshape(n)


def _dispatch(dest_by_choice, x2p, n_rows):
    N, W = x2p.shape
    workers = SC_CORES * SC_SUBCORES
    per_worker = N // workers
    mesh = plsc.VectorSubcoreMesh(core_axis_name="core", subcore_axis_name="subcore",
                                  num_cores=SC_CORES, num_subcores=SC_SUBCORES)

    @functools.partial(
        pl.kernel, mesh=mesh,
        out_type=jax.ShapeDtypeStruct((n_rows, W), jnp.uint32),
        scratch_types=[pltpu.VMEM((SC_CHUNK,), jnp.int32), pltpu.VMEM((SC_CHUNK, W), jnp.uint32)],
        name="dispatch",
    )
    def scatter(x_hbm, dest_hbm, rows_hbm, idx_v, rows_v):
        worker = lax.axis_index("subcore") * SC_CORES + lax.axis_index("core")

        @pl.loop(0, per_worker // SC_CHUNK)
        def _(c):
            base = worker * per_worker + c * SC_CHUNK
            pltpu.sync_copy(x_hbm.at[pl.ds(base, SC_CHUNK)], rows_v)
            for k in range(TOP_K):
                pltpu.sync_copy(dest_hbm.at[k, pl.ds(base, SC_CHUNK)], idx_v)
                pltpu.sync_copy(rows_v, rows_hbm.at[idx_v])

    return scatter(x2p, dest_by_choice)


def _experts_kernel(layer, be_ref, first_ref, nvalid_ref, next_ref, slot_ref, used_ref,
                    rows_ref, bgu_ref, bd_ref, wgu_hbm, wd_hbm,
                    y_ref, wgu_f32, wd_f32, wgu_bf, wd_bf, sem):
    i = pl.program_id(0)

    def fetch(e, slot):
        return (pltpu.make_async_copy(wgu_hbm.at[layer, e], wgu_f32.at[slot], sem.at[0, slot]),
                pltpu.make_async_copy(wd_hbm.at[layer, e], wd_f32.at[slot], sem.at[1, slot]))

    @pl.when(i < nvalid_ref[0])
    def _():
        @pl.when(first_ref[i] == 1)
        def _():
            slot = slot_ref[i]

            @pl.when(i == 0)
            def _():
                for c in fetch(be_ref[i], slot):
                    c.start()

            for c in fetch(be_ref[i], slot):
                c.wait()
            wgu_bf[...] = wgu_f32[slot].astype(BF16)
            wd_bf[...] = wd_f32[slot].astype(BF16)

            @pl.when(next_ref[i] >= 0)
            def _():
                for c in fetch(next_ref[i], 1 - slot):
                    c.start()

        live = lax.broadcasted_iota(jnp.int32, rows_ref.shape, 0) < used_ref[i]
        lo, hi = _unpack_bf16_pairs(jnp.where(live, rows_ref[...], jnp.uint32(0)))
        half = D_MODEL // 2
        hgu = _dot(lo, wgu_bf[:half, :]) + _dot(hi, wgu_bf[half:, :]) + bgu_ref[...]
        g = jnp.minimum(hgu[:, :D_MODEL], SWIGLU_LIMIT)
        u = jnp.clip(hgu[:, D_MODEL:], -SWIGLU_LIMIT, SWIGLU_LIMIT)
        a = (u + 1.0) * (g * jax.nn.sigmoid(SWIGLU_ALPHA * g))
        y_ref[...] = _dot(a.astype(BF16), wd_bf[...]) + bd_ref[...]

    @pl.when(i >= nvalid_ref[0])
    def _():
        y_ref[...] = jnp.zeros_like(y_ref)


def _experts(layer, plan, rows, w_gu, b_gu, w_down, b_down):
    P, W = rows.shape
    L, E, D, F2 = w_gu.shape
    nb = P // BLK_E

    def row_map(i, be, bf, nv, nx, sl, us):
        return (jnp.minimum(i, nv[0] - 1), 0)

    def b_map(i, be, bf, nv, nx, sl, us):
        return (layer, be[i], 0, 0)

    grid_spec = pltpu.PrefetchScalarGridSpec(
        num_scalar_prefetch=6,
        grid=(nb,),
        in_specs=[pl.BlockSpec((BLK_E, W), row_map),
                  pl.BlockSpec((None, None, 1, F2), b_map),
                  pl.BlockSpec((None, None, 1, D), b_map),
                  pl.BlockSpec(memory_space=pl.ANY),
                  pl.BlockSpec(memory_space=pl.ANY)],
        out_specs=pl.BlockSpec((BLK_E, D), lambda i, *_: (i, 0)),
        scratch_shapes=[pltpu.VMEM((2, D, F2), F32), pltpu.VMEM((2, F2 // 2, D), F32),
                        pltpu.VMEM((D, F2), BF16), pltpu.VMEM((F2 // 2, D), BF16),
                        pltpu.SemaphoreType.DMA((2, 2))],
    )
    return pl.pallas_call(
        functools.partial(_experts_kernel, layer),
        out_shape=jax.ShapeDtypeStruct((P, D), F32),
        grid_spec=grid_spec,
        compiler_params=pltpu.CompilerParams(
            dimension_semantics=("arbitrary",), vmem_limit_bytes=VMEM_LIMIT),
        name="experts",
    )(plan.block_e, plan.block_first, plan.nvalid, plan.next_e, plan.slot, plan.block_rows, rows,
      b_gu.reshape(L, E, 1, F2), b_down.reshape(L, E, 1, D), w_gu, w_down)


def _combine_kernel(dest_ref, dest_next_ref, gate_ref, x2_ref, lng_ref, lnb_ref, y_ref, o_ref,
                    ybuf, sem):
    i = pl.program_id(0)
    slot = i & 1

    def gather(d_ref, buf):
        def start(t, carry):
            for k in range(TOP_K):
                _row_copy(y_ref, d_ref[t * TOP_K + k], ybuf.at[buf, k], t,
                          sem.at[buf]).start(priority=k % 2)
            return carry

        lax.fori_loop(0, TD_ROWS, start, 0, unroll=DMA_UNROLL)

    pl.when(i == 0)(lambda: gather(dest_ref, slot))
    pl.when(i + 1 < pl.num_programs(0))(lambda: gather(dest_next_ref, 1 - slot))
    for k in range(TOP_K):
        pltpu.make_async_copy(y_ref.at[pl.ds(0, TD_ROWS)], ybuf.at[slot, k], sem.at[slot]).wait()

    gates = gate_ref[...]
    ff = gates[:, 0:1] * ybuf[slot, 0]
    for k in range(1, TOP_K):
        ff = ff + gates[:, k:k + 1] * ybuf[slot, k]
    alpha = (2 * 2) ** 0.25
    o_ref[...] = _layer_norm(alpha * x2_ref[...] + ff, lng_ref[2:3, :], lnb_ref[2:3, :])


def _combine(dest_flat, gates, x2, ln_g, ln_b, y):
    N, D = x2.shape
    steps = N // TD_ROWS
    dest_blk = lambda ahead: pl.BlockSpec(
        (TD_ROWS * TOP_K,), lambda i: (jnp.minimum(i + ahead, steps - 1),), memory_space=pltpu.SMEM)
    return pl.pallas_call(
        _combine_kernel,
        out_shape=jax.ShapeDtypeStruct((N, D), F32),
        grid=(steps,),
        in_specs=[dest_blk(0), dest_blk(1),
                  pl.BlockSpec((TD_ROWS, TOP_K), lambda i: (i, 0)),
                  pl.BlockSpec((TD_ROWS, D), lambda i: (i, 0)),
                  pl.BlockSpec(ln_g.shape, lambda i: (0, 0)),
                  pl.BlockSpec(ln_b.shape, lambda i: (0, 0)),
                  pl.BlockSpec(memory_space=pl.ANY)],
        out_specs=pl.BlockSpec((TD_ROWS, D), lambda i: (i, 0)),
        scratch_shapes=[pltpu.VMEM((2, TOP_K, TD_ROWS, D), F32), pltpu.SemaphoreType.DMA((2,))],
        compiler_params=pltpu.CompilerParams(
            dimension_semantics=("arbitrary",), vmem_limit_bytes=VMEM_LIMIT),
        name="combine",
    )(dest_flat, dest_flat, gates, x2, ln_g, ln_b, y)


class _Plan(NamedTuple):
    pstart: jax.Array
    bend: jax.Array
    nvalid: jax.Array
    block_e: jax.Array
    block_first: jax.Array
    next_e: jax.Array
    slot: jax.Array
    block_rows: jax.Array


def _routing_plan(counts, nb):
    counts = counts.reshape(N_EXPERTS).astype(jnp.int32)
    nblk = (counts + BLK_E - 1) // BLK_E
    bend = jnp.cumsum(nblk)
    pstart = (bend - nblk) * BLK_E
    nvalid = bend[-1:]
    blocks = jnp.minimum(jnp.arange(nb, dtype=jnp.int32), nvalid[0] - 1)
    block_e = jnp.sum((blocks[:, None] >= bend[None, :]).astype(jnp.int32), axis=1)
    block_e = jnp.minimum(block_e, N_EXPERTS - 1)
    block_first = jnp.concatenate(
        [jnp.ones((1,), jnp.int32), (block_e[1:] != block_e[:-1]).astype(jnp.int32)])
    experts = jnp.arange(N_EXPERTS, dtype=jnp.int32)
    at_or_after = lax.cummin(jnp.where(nblk > 0, experts, N_EXPERTS), reverse=True)
    after = jnp.concatenate([at_or_after[1:], jnp.full((1,), N_EXPERTS, jnp.int32)])
    next_e = jnp.where(after < N_EXPERTS, after, -1)[block_e]
    slot = (jnp.cumsum(block_first) - 1) & 1
    rows_before = (blocks - (bend - nblk)[block_e]) * BLK_E
    block_rows = jnp.clip(counts[block_e] - rows_before, 0, BLK_E)
    return _Plan(pstart, bend, nvalid, block_e, block_first, next_e.astype(jnp.int32),
                 slot.astype(jnp.int32), block_rows.astype(jnp.int32))


def _layer(layer, x, mem, w_in, sgu_g, sgu_b, w_sp, b_sp, grp_g, w_out, wq, wkv, wo,
           w_router, b_router, w_gu, b_gu, w_down, b_down, ln_g, ln_b):
    B, S, D = x.shape
    N = B * S
    row = lambda a: a.reshape(1, -1)

    ya, q, k, v = _mixer_in(x, w_in.astype(BF16), row(sgu_g), row(sgu_b), w_sp,
                            b_sp.reshape(SGU_GROUPS, CHUNK, 1), row(grp_g[:SGU_WIDTH]))
    yb = _sb_attn(q, k, v, row(grp_g[SGU_WIDTH:]))
    k_mem, v_mem = _kv_proj(mem, wkv.astype(BF16))
    x2, x2p, idx, rank, gates, counts = _post_mixer(
        x, ya, yb, w_out.astype(BF16), ln_g, ln_b, wq.astype(BF16), k_mem, v_mem,
        wo.astype(BF16), w_router, row(b_router))

    nb = (N * TOP_K) // BLK_E + N_EXPERTS
    plan = _routing_plan(counts, nb)
    dest = _dest_rows(plan.pstart, idx.reshape(-1), rank.reshape(-1))
    rows = _dispatch(dest.reshape(N, TOP_K).T, x2p.reshape(N, D // 2), nb * BLK_E)
    y = _experts(layer, plan, rows, w_gu, b_gu, w_down, b_down)
    out = _combine(dest, gates.reshape(N, TOP_K), x2.reshape(N, D), ln_g, ln_b, y)
    return out.reshape(B, S, D)


def kernel(x, mem, w_in, sgu_g, sgu_b, w_sp, b_sp, grp_g, w_out, wq_mem, wkv_mem, wo_mem,
           w_router, b_router, w_gu, b_gu, w_down, b_down, ln_g, ln_b):
    for l in range(w_in.shape[0]):
        x = _layer(l, x, mem, w_in[l], sgu_g[l], sgu_b[l], w_sp[l], b_sp[l], grp_g[l], w_out[l],
                   wq_mem[l], wkv_mem[l], wo_mem[l], w_router[l], b_router[l], w_gu, b_gu,
                   w_down, b_down, ln_g[l], ln_b[l])
    return x
```

```python
import functools
from typing import NamedTuple

import jax
import jax.numpy as jnp
from jax import lax
from jax.experimental import pallas as pl
from jax.experimental.pallas import tpu as pltpu
from jax.experimental.pallas import tpu_sc as plsc

F32 = jnp.float32
BF16 = jnp.bfloat16

D_MODEL = 1024
SGU_WIDTH = 512
SGU_GROUPS = 4
GROUP_DIM = 128
CHUNK = 128
SB_WIDTH = 512
SB_HEADS = 8
SB_HEAD_DIM = 64
MEM_HEADS = 4
MEM_HEAD_DIM = 256
N_EXPERTS = 32
TOP_K = 4
SWIGLU_LIMIT = 7.0
SWIGLU_ALPHA = 1.702
LN_EPS = 1e-5
RMS_EPS = 1e-6
LOG2E = 1.4426950408889634

LANES = 128
TM_MIX = 1024
TQ_SB = 512
TM_POST = 1024
TD_ROWS = 256
BLK_E = 256
SC_CORES = 2
SC_SUBCORES = 16
SC_CHUNK = 64
VMEM_LIMIT = 56 * 1024 * 1024


def _layer_norm(x, g, b):
    mu = jnp.mean(x, axis=-1, keepdims=True)
    xc = x - mu
    var = jnp.mean(xc * xc, axis=-1, keepdims=True)
    return xc * lax.rsqrt(var + LN_EPS) * g + b


def _dot(a, b):
    return jnp.dot(a, b, preferred_element_type=F32)


def _dot_nt(a, b):
    return lax.dot_general(a, b, (((1,), (1,)), ((), ())), preferred_element_type=F32)


def _mixer_in_kernel(x_ref, win_ref, sg_ref, sb_ref, wsp_ref, bsp_ref, gg_ref,
                     ya_ref, q_ref, k_ref, v_ref):
    xb = x_ref[...].astype(BF16)

    def proj(lo, hi):
        return _dot(xb, win_ref[:, lo:hi])

    u = jax.nn.gelu(proj(0, SGU_WIDTH))
    vs = jax.nn.gelu(proj(SGU_WIDTH, 2 * SGU_WIDTH))
    r = lax.broadcasted_iota(jnp.int32, (CHUNK, CHUNK), 0)
    c = lax.broadcasted_iota(jnp.int32, (CHUNK, CHUNK), 1)
    causal = r >= c
    for g in range(SGU_GROUPS):
        gs = slice(g * GROUP_DIM, (g + 1) * GROUP_DIM)
        vn = _layer_norm(vs[:, gs], sg_ref[:, gs], sb_ref[:, gs]).astype(BF16)
        wc = jnp.where(causal, wsp_ref[g], 0.0).astype(BF16)
        for ch in range(TM_MIX // CHUNK):
            cs = slice(ch * CHUNK, (ch + 1) * CHUNK)
            gate = _dot(wc, vn[cs]) + bsp_ref[g]
            oa = u[cs, gs] * gate
            ms = jnp.mean(oa * oa, axis=-1, keepdims=True)
            ya_ref[cs, gs] = (oa * lax.rsqrt(ms + RMS_EPS) * gg_ref[:, gs]).astype(BF16)
    o = 2 * SGU_WIDTH
    q_ref[...] = (proj(o, o + SB_WIDTH) * (SB_HEAD_DIM ** -0.5)).astype(BF16)
    k_ref[...] = proj(o + SB_WIDTH, o + 2 * SB_WIDTH).astype(BF16)
    v_ref[...] = proj(o + 2 * SB_WIDTH, o + 3 * SB_WIDTH).astype(BF16)


def _mixer_in(x, w_in, sgu_g, sgu_b, w_sp, b_sp, grp_g_a):
    B, S, D = x.shape
    full = lambda shape: pl.BlockSpec(shape, lambda b, i: (0,) * len(shape))
    tile = lambda w: pl.BlockSpec((None, TM_MIX, w), lambda b, i: (b, i, 0))
    out = jax.ShapeDtypeStruct((B, S, SGU_WIDTH), BF16)
    return pl.pallas_call(
        _mixer_in_kernel,
        out_shape=(out, out, out, out),
        grid=(B, S // TM_MIX),
        in_specs=[tile(D), full(w_in.shape), full(sgu_g.shape), full(sgu_b.shape),
                  full(w_sp.shape), full(b_sp.shape), full(grp_g_a.shape)],
        out_specs=(tile(SGU_WIDTH),) * 4,
        compiler_params=pltpu.CompilerParams(
            dimension_semantics=("parallel", "parallel"), vmem_limit_bytes=VMEM_LIMIT),
        name="mixer_in",
    )(x, w_in, sgu_g, sgu_b, w_sp, b_sp, grp_g_a)


def _sb_attn_kernel(q_ref, k_ref, v_ref, gg_ref, o_ref):
    i = pl.program_id(2)
    lo = i * TQ_SB
    lane = lax.broadcasted_iota(jnp.int32, (1, LANES), 1)
    head0 = lane < SB_HEAD_DIM
    q_all = q_ref[...]
    r = lax.broadcasted_iota(jnp.int32, (2 * LANES, 2 * LANES), 0)
    c = lax.broadcasted_iota(jnp.int32, (2 * LANES, 2 * LANES), 1)
    after = jnp.where((r > c) & ((r // LANES) == (c // LANES)), 1.0, 0.0).astype(BF16)

    def split_heads(x):
        zx = jnp.zeros_like(x)
        return jnp.concatenate([jnp.where(head0, x, zx), jnp.where(head0, zx, x)], axis=0)

    earlier = ((lax.broadcasted_iota(jnp.int32, (LANES, 2 * LANES), 1) & (LANES - 1))
               < lax.broadcasted_iota(jnp.int32, (LANES, 2 * LANES), 0))

    def keep_earlier_keys(x):
        top = jnp.where(earlier, x[:LANES], 0.0)
        return jnp.concatenate([top, x[LANES:]], axis=0) if x.shape[0] > LANES else top

    def add_rows(x, r0, upd):
        new = x[r0:] + upd
        return jnp.concatenate([x[:r0], new], axis=0) if r0 else new

    def key_blocks(blocks, carry, acc):
        starts = [pl.multiple_of(s0, LANES) for _, s0, _ in blocks]
        k2s = [split_heads(k_ref[pl.ds(s, LANES), :]) for s in starts]
        v2s = [split_heads(v_ref[pl.ds(s, LANES), :]) for s in starts]
        zs = [_dot_nt(q_all[r0:], k2) for (r0, _, _), k2 in zip(blocks, k2s)]
        sps = []
        for (_, _, causal), z in zip(blocks, zs):
            sp = jnp.maximum(z, 0.0) + jnp.log(1.0 + jnp.exp2(jnp.abs(z) * (-LOG2E)))
            if causal:
                sp = keep_earlier_keys(sp)
            sps.append(sp)
        his = [sp.astype(BF16) for sp in sps]
        los = [(sp - hi.astype(F32)).astype(BF16) for sp, hi in zip(sps, his)]
        laters = [_dot(hi, after) + _dot(lo_part, after) for hi, lo_part in zip(his, los)]
        weights = []
        for (r0, _, causal), z, sp, later in zip(blocks, zs, sps, laters):
            a = jnp.exp(z - (sp + later + carry[r0:]))
            if causal:
                a = keep_earlier_keys(a)
            weights.append(a.astype(BF16))
            rows = [jnp.broadcast_to(jnp.sum(sp[:, h * LANES:(h + 1) * LANES], axis=1, keepdims=True),
                                     (z.shape[0], LANES)) for h in range(2)]
            carry = add_rows(carry, r0, jnp.concatenate(rows, axis=1))
        for (r0, _, _), a, v2 in zip(blocks, weights, v2s):
            acc = add_rows(acc, r0, _dot(a, v2))
        return carry, acc

    nsub = TQ_SB // LANES
    carry = jnp.zeros((TQ_SB, 2 * LANES), F32)
    acc = jnp.zeros((TQ_SB, LANES), F32)
    carry, acc = key_blocks([(sb * LANES, lo + sb * LANES, True) for sb in reversed(range(nsub))],
                            carry, acc)

    def body(j, st):
        s0 = pl.multiple_of(lo - (j + 1) * TQ_SB, TQ_SB)
        return key_blocks([(0, s0 + sb * LANES, False) for sb in reversed(range(nsub))], *st)

    carry, acc = lax.fori_loop(0, i, body, (carry, acc))

    sq = acc * acc
    ss0 = jnp.sum(jnp.where(head0, sq, 0.0), axis=1, keepdims=True)
    ss1 = jnp.sum(jnp.where(head0, 0.0, sq), axis=1, keepdims=True)
    inv = 1.0 / SB_HEAD_DIM
    rs = jnp.where(head0, lax.rsqrt(ss0 * inv + RMS_EPS), lax.rsqrt(ss1 * inv + RMS_EPS))
    o_ref[...] = (acc * rs * gg_ref[...]).astype(BF16)


def _sb_attn(q, k, v, grp_g_b):
    B, S, W = q.shape
    pairs = W // LANES
    return pl.pallas_call(
        _sb_attn_kernel,
        out_shape=jax.ShapeDtypeStruct((B, S, W), BF16),
        grid=(B, pairs, S // TQ_SB),
        in_specs=[pl.BlockSpec((None, TQ_SB, LANES), lambda b, p, i: (b, i, p)),
                  pl.BlockSpec((None, S, LANES), lambda b, p, i: (b, 0, p)),
                  pl.BlockSpec((None, S, LANES), lambda b, p, i: (b, 0, p)),
                  pl.BlockSpec((1, LANES), lambda b, p, i: (0, p))],
        out_specs=pl.BlockSpec((None, TQ_SB, LANES), lambda b, p, i: (b, i, p)),
        compiler_params=pltpu.CompilerParams(
            dimension_semantics=("parallel", "parallel", "arbitrary"),
            vmem_limit_bytes=VMEM_LIMIT),
        name="sb_attn",
    )(q, k, v, grp_g_b)


def _kv_proj_kernel(mem_ref, wkv_ref, k_ref, v_ref):
    mb = mem_ref[...].astype(BF16)
    k_ref[...] = _dot(mb, wkv_ref[:, :D_MODEL]).astype(BF16)
    v_ref[...] = _dot(mb, wkv_ref[:, D_MODEL:]).astype(BF16)


def _kv_proj(mem, wkv):
    B, M, D = mem.shape
    out = jax.ShapeDtypeStruct((B, M, D), BF16)
    return pl.pallas_call(
        _kv_proj_kernel,
        out_shape=(out, out),
        grid=(B,),
        in_specs=[pl.BlockSpec((None, M, D), lambda b: (b, 0, 0)),
                  pl.BlockSpec(wkv.shape, lambda b: (0, 0))],
        out_specs=(pl.BlockSpec((None, M, D), lambda b: (b, 0, 0)),) * 2,
        compiler_params=pltpu.CompilerParams(
            dimension_semantics=("parallel",), vmem_limit_bytes=VMEM_LIMIT),
        name="kv_proj",
    )(mem, wkv)


def _pack_bf16_pairs(x):
    w = x.shape[1] // 2
    bits = lax.bitcast_convert_type(x.astype(BF16).astype(F32), jnp.uint32)
    return (bits[:, w:] & jnp.uint32(0xFFFF0000)) | (bits[:, :w] >> 16)


def _unpack_bf16_pairs(u):
    lo = lax.bitcast_convert_type(u << 16, F32).astype(BF16)
    hi = lax.bitcast_convert_type(u & jnp.uint32(0xFFFF0000), F32).astype(BF16)
    return lo, hi


def _post_mixer_kernel(x_ref, ya_ref, yb_ref, wout_ref, lng_ref, lnb_ref, wq_ref, km_ref, vm_ref,
                       wo_ref, wr_ref, br_ref,
                       x2_ref, x2p_ref, idx_ref, rank_ref, gate_ref, cnt_ref, carry_ref):
    first = jnp.logical_and(pl.program_id(0) == 0, pl.program_id(1) == 0)

    @pl.when(first)
    def _():
        carry_ref[...] = jnp.zeros_like(carry_ref)

    alpha = (2 * 2) ** 0.25
    mix = _dot(ya_ref[...], wout_ref[:SGU_WIDTH, :]) + _dot(yb_ref[...], wout_ref[SGU_WIDTH:, :])
    x1 = _layer_norm(alpha * x_ref[...] + mix, lng_ref[0:1, :], lnb_ref[0:1, :])

    qm = (_dot(x1.astype(BF16), wq_ref[...]) * (MEM_HEAD_DIM ** -0.5)).astype(BF16)
    heads = []
    for h in range(MEM_HEADS):
        hs = slice(h * MEM_HEAD_DIM, (h + 1) * MEM_HEAD_DIM)
        s = _dot_nt(qm[:, hs], km_ref[:, hs])
        e = jnp.exp(s - jnp.max(s, axis=-1, keepdims=True))
        p = (e / jnp.sum(e, axis=-1, keepdims=True)).astype(BF16)
        heads.append(_dot(p, vm_ref[:, hs]).astype(BF16))
    o = jnp.concatenate(heads, axis=1)
    x2 = _layer_norm(alpha * x1 + _dot(o, wo_ref[...]), lng_ref[1:2, :], lnb_ref[1:2, :])
    x2_ref[...] = x2
    x2p_ref[...] = _pack_bf16_pairs(x2)

    xh = x2.astype(BF16)
    xl = (x2 - xh.astype(F32)).astype(BF16)
    wr = wr_ref[...]
    wh = wr.astype(BF16)
    wl = (wr - wh.astype(F32)).astype(BF16)
    logits = _dot(xh, wh) + (_dot(xl, wh) + _dot(xh, wl)) + br_ref[...]

    tm = logits.shape[0]
    lane = lax.broadcasted_iota(jnp.int32, (tm, N_EXPERTS), 1).astype(F32)
    work = logits
    vals, idxs = [], []
    chosen = jnp.zeros((tm, N_EXPERTS), F32)
    for _ in range(TOP_K):
        m = jnp.max(work, axis=-1, keepdims=True)
        ix = jnp.min(jnp.where(work == m, lane, float(N_EXPERTS)), axis=-1, keepdims=True)
        sel = lane == ix
        work = jnp.where(sel, -jnp.inf, work)
        chosen = jnp.where(sel, 1.0, chosen)
        vals.append(m)
        idxs.append(ix)
    ev = [jnp.exp(v - vals[0]) for v in vals]
    den = ev[0] + ev[1] + ev[2] + ev[3]

    r = lax.broadcasted_iota(jnp.int32, (tm, tm), 0)
    c = lax.broadcasted_iota(jnp.int32, (tm, tm), 1)
    before = jnp.where(r > c, 1.0, 0.0).astype(BF16)
    running = _dot(before, chosen.astype(BF16)) + carry_ref[...]
    ranks = [jnp.sum(jnp.where(lane == ix, running, 0.0), axis=-1, keepdims=True) for ix in idxs]
    carry_ref[...] = carry_ref[...] + jnp.sum(chosen, axis=0, keepdims=True)
    cnt_ref[...] = carry_ref[...]

    l4 = lax.broadcasted_iota(jnp.int32, (tm, TOP_K), 1)

    def cols(xs):
        return jnp.where(l4 == 0, xs[0], jnp.where(l4 == 1, xs[1], jnp.where(l4 == 2, xs[2], xs[3])))

    idx_ref[...] = cols(idxs).astype(jnp.int32)
    rank_ref[...] = cols(ranks).astype(jnp.int32)
    gate_ref[...] = cols([e / den for e in ev])


def _post_mixer(x, ya, yb, w_out, ln_g, ln_b, wq, k_mem, v_mem, wo, w_router, b_router):
    B, S, D = x.shape
    M = k_mem.shape[1]
    full = lambda a: pl.BlockSpec(a.shape, lambda b, i: (0,) * a.ndim)
    tile = lambda w: pl.BlockSpec((None, TM_POST, w), lambda b, i: (b, i, 0))
    memb = pl.BlockSpec((None, M, D), lambda b, i: (b, 0, 0))
    sds = jax.ShapeDtypeStruct
    return pl.pallas_call(
        _post_mixer_kernel,
        out_shape=(sds((B, S, D), F32), sds((B, S, D // 2), jnp.uint32),
                   sds((B, S, TOP_K), jnp.int32), sds((B, S, TOP_K), jnp.int32),
                   sds((B, S, TOP_K), F32), sds((1, N_EXPERTS), F32)),
        grid=(B, S // TM_POST),
        in_specs=[tile(D), tile(SGU_WIDTH), tile(SB_WIDTH), full(w_out), full(ln_g), full(ln_b),
                  full(wq), memb, memb, full(wo), full(w_router), full(b_router)],
        out_specs=(tile(D), tile(D // 2), tile(TOP_K), tile(TOP_K), tile(TOP_K),
                   pl.BlockSpec((1, N_EXPERTS), lambda b, i: (0, 0))),
        scratch_shapes=[pltpu.VMEM((1, N_EXPERTS), F32)],
        compiler_params=pltpu.CompilerParams(
            dimension_semantics=("arbitrary", "arbitrary"), vmem_limit_bytes=VMEM_LIMIT),
        name="post_mixer",
    )(x, ya, yb, w_out, ln_g, ln_b, wq, k_mem, v_mem, wo, w_router, b_router)


def _dest_rows_kernel(pstart_ref, idx_ref, rank_ref, dest_ref):
    idx = idx_ref[...]
    base = jnp.zeros_like(idx)
    for e in range(N_EXPERTS):
        base = jnp.where(idx == e, pstart_ref[e], base)
    dest_ref[...] = base + rank_ref[...]


def _dest_rows(pstart, idx_flat, rank_flat):
    n = idx_flat.shape[0]
    shape2d = (n // LANES, LANES)
    full = pl.BlockSpec(shape2d, lambda i, ps: (0, 0))
    dest = pl.pallas_call(
        _dest_rows_kernel,
        out_shape=jax.ShapeDtypeStruct(shape2d, jnp.int32),
        grid_spec=pltpu.PrefetchScalarGridSpec(
            num_scalar_prefetch=1, grid=(1,), in_specs=[full, full], out_specs=full),
        name="dest_rows",
    )(pstart, idx_flat.reshape(shape2d), rank_flat.reshape(shape2d))
    return dest.reshape(n)


def _dispatch(dest_by_choice, x2p, n_rows):
    N, W = x2p.shape
    workers = SC_CORES * SC_SUBCORES
    per_worker = N // workers
    mesh = plsc.VectorSubcoreMesh(core_axis_name="core", subcore_axis_name="subcore",
                                  num_cores=SC_CORES, num_subcores=SC_SUBCORES)

    @functools.partial(
        pl.kernel, mesh=mesh,
        out_type=jax.ShapeDtypeStruct((n_rows, W), jnp.uint32),
        scratch_types=[pltpu.VMEM((SC_CHUNK,), jnp.int32), pltpu.VMEM((SC_CHUNK, W), jnp.uint32)],
        name="dispatch",
    )
    def scatter(x_hbm, dest_hbm, rows_hbm, idx_v, rows_v):
        worker = lax.axis_index("subcore") * SC_CORES + lax.axis_index("core")

        @pl.loop(0, per_worker // SC_CHUNK)
        def _(c):
            base = worker * per_worker + c * SC_CHUNK
            pltpu.sync_copy(x_hbm.at[pl.ds(base, SC_CHUNK)], rows_v)
            for k in range(TOP_K):
                pltpu.sync_copy(dest_hbm.at[k, pl.ds(base, SC_CHUNK)], idx_v)
                pltpu.sync_copy(rows_v, rows_hbm.at[idx_v])

    return scatter(x2p, dest_by_choice)


def _experts_kernel(layer, be_ref, first_ref, nvalid_ref, next_ref, slot_ref, used_ref,
                    rows_ref, bgu_ref, bd_ref, wgu_hbm, wd_hbm,
                    y_ref, wgu_f32, wd_f32, wgu_bf, wd_bf, sem):
    i = pl.program_id(0)

    def fetch(e, slot):
        return (pltpu.make_async_copy(wgu_hbm.at[layer, e], wgu_f32.at[slot], sem.at[0, slot]),
                pltpu.make_async_copy(wd_hbm.at[layer, e], wd_f32.at[slot], sem.at[1, slot]))

    @pl.when(i < nvalid_ref[0])
    def _():
        @pl.when(first_ref[i] == 1)
        def _():
            slot = slot_ref[i]

            @pl.when(i == 0)
            def _():
                for c in fetch(be_ref[i], slot):
                    c.start()

            for c in fetch(be_ref[i], slot):
                c.wait()
            wgu_bf[...] = wgu_f32[slot].astype(BF16)
            wd_bf[...] = wd_f32[slot].astype(BF16)

            @pl.when(next_ref[i] >= 0)
            def _():
                for c in fetch(next_ref[i], 1 - slot):
                    c.start()

        live = lax.broadcasted_iota(jnp.int32, rows_ref.shape, 0) < used_ref[i]
        lo, hi = _unpack_bf16_pairs(jnp.where(live, rows_ref[...], jnp.uint32(0)))
        half = D_MODEL // 2
        hgu = _dot(lo, wgu_bf[:half, :]) + _dot(hi, wgu_bf[half:, :]) + bgu_ref[...]
        g = jnp.minimum(hgu[:, :D_MODEL], SWIGLU_LIMIT)
        u = jnp.clip(hgu[:, D_MODEL:], -SWIGLU_LIMIT, SWIGLU_LIMIT)
        a = (u + 1.0) * (g * jax.nn.sigmoid(SWIGLU_ALPHA * g))
        y_ref[...] = _dot(a.astype(BF16), wd_bf[...]) + bd_ref[...]

    @pl.when(i >= nvalid_ref[0])
    def _():
        y_ref[...] = jnp.zeros_like(y_ref)


def _experts(layer, plan, rows, w_gu, b_gu, w_down, b_down):
    P, W = rows.shape
    L, E, D, F2 = w_gu.shape
    nb = P // BLK_E

    def row_map(i, be, bf, nv, nx, sl, us):
        return (jnp.minimum(i, nv[0] - 1), 0)

    def b_map(i, be, bf, nv, nx, sl, us):
        return (layer, be[i], 0, 0)

    grid_spec = pltpu.PrefetchScalarGridSpec(
        num_scalar_prefetch=6,
        grid=(nb,),
        in_specs=[pl.BlockSpec((BLK_E, W), row_map),
                  pl.BlockSpec((None, None, 1, F2), b_map),
                  pl.BlockSpec((None, None, 1, D), b_map),
                  pl.BlockSpec(memory_space=pl.ANY),
                  pl.BlockSpec(memory_space=pl.ANY)],
        out_specs=pl.BlockSpec((BLK_E, D), lambda i, *_: (i, 0)),
        scratch_shapes=[pltpu.VMEM((2, D, F2), F32), pltpu.VMEM((2, F2 // 2, D), F32),
                        pltpu.VMEM((D, F2), BF16), pltpu.VMEM((F2 // 2, D), BF16),
                        pltpu.SemaphoreType.DMA((2, 2))],
    )
    return pl.pallas_call(
        functools.partial(_experts_kernel, layer),
        out_shape=jax.ShapeDtypeStruct((P, D), F32),
        grid_spec=grid_spec,
        compiler_params=pltpu.CompilerParams(
            dimension_semantics=("arbitrary",), vmem_limit_bytes=VMEM_LIMIT),
        name="experts",
    )(plan.block_e, plan.block_first, plan.nvalid, plan.next_e, plan.slot, plan.block_rows, rows,
      b_gu.reshape(L, E, 1, F2), b_down.reshape(L, E, 1, D), w_gu, w_down)


def _gather_expert_rows(dest_by_choice, y):
    n_choice, N = dest_by_choice.shape
    D = y.shape[1]
    workers = SC_CORES * SC_SUBCORES
    per_worker = N // workers
    mesh = plsc.VectorSubcoreMesh(core_axis_name="core", subcore_axis_name="subcore",
                                  num_cores=SC_CORES, num_subcores=SC_SUBCORES)

    @functools.partial(
        pl.kernel, mesh=mesh,
        out_type=jax.ShapeDtypeStruct((n_choice, N, D), y.dtype),
        scratch_types=[pltpu.VMEM((SC_CHUNK,), jnp.int32), pltpu.VMEM((SC_CHUNK, D), y.dtype)],
        name="gather_expert_rows",
    )
    def gather(y_hbm, dest_hbm, out_hbm, idx_v, rows_v):
        worker = lax.axis_index("subcore") * SC_CORES + lax.axis_index("core")

        @pl.loop(0, per_worker // SC_CHUNK)
        def _(c):
            base = worker * per_worker + c * SC_CHUNK
            for k in range(n_choice):
                pltpu.sync_copy(dest_hbm.at[k, pl.ds(base, SC_CHUNK)], idx_v)
                pltpu.sync_copy(y_hbm.at[idx_v], rows_v)
                pltpu.sync_copy(rows_v, out_hbm.at[k, pl.ds(base, SC_CHUNK)])

    return gather(y, dest_by_choice)


def _combine_kernel(gate_ref, x2_ref, lng_ref, lnb_ref, yk_ref, o_ref):
    gates = gate_ref[...]
    ff = gates[:, 0:1] * yk_ref[0]
    for k in range(1, TOP_K):
        ff = ff + gates[:, k:k + 1] * yk_ref[k]
    alpha = (2 * 2) ** 0.25
    o_ref[...] = _layer_norm(alpha * x2_ref[...] + ff, lng_ref[2:3, :], lnb_ref[2:3, :])


def _combine(gates, x2, ln_g, ln_b, y_by_choice):
    N, D = x2.shape
    return pl.pallas_call(
        _combine_kernel,
        out_shape=jax.ShapeDtypeStruct((N, D), F32),
        grid=(N // TD_ROWS,),
        in_specs=[pl.BlockSpec((TD_ROWS, TOP_K), lambda i: (i, 0)),
                  pl.BlockSpec((TD_ROWS, D), lambda i: (i, 0)),
                  pl.BlockSpec(ln_g.shape, lambda i: (0, 0)),
                  pl.BlockSpec(ln_b.shape, lambda i: (0, 0)),
                  pl.BlockSpec((TOP_K, TD_ROWS, D), lambda i: (0, i, 0))],
        out_specs=pl.BlockSpec((TD_ROWS, D), lambda i: (i, 0)),
        compiler_params=pltpu.CompilerParams(
            dimension_semantics=("parallel",), vmem_limit_bytes=VMEM_LIMIT),
        name="combine",
    )(gates, x2, ln_g, ln_b, y_by_choice)


class _Plan(NamedTuple):
    pstart: jax.Array
    bend: jax.Array
    nvalid: jax.Array
    block_e: jax.Array
    block_first: jax.Array
    next_e: jax.Array
    slot: jax.Array
    block_rows: jax.Array


def _routing_plan(counts, nb):
    counts = counts.reshape(N_EXPERTS).astype(jnp.int32)
    nblk = (counts + BLK_E - 1) // BLK_E
    bend = jnp.cumsum(nblk)
    pstart = (bend - nblk) * BLK_E
    nvalid = bend[-1:]
    blocks = jnp.minimum(jnp.arange(nb, dtype=jnp.int32), nvalid[0] - 1)
    block_e = jnp.sum((blocks[:, None] >= bend[None, :]).astype(jnp.int32), axis=1)
    block_e = jnp.minimum(block_e, N_EXPERTS - 1)
    block_first = jnp.concatenate(
        [jnp.ones((1,), jnp.int32), (block_e[1:] != block_e[:-1]).astype(jnp.int32)])
    experts = jnp.arange(N_EXPERTS, dtype=jnp.int32)
    at_or_after = lax.cummin(jnp.where(nblk > 0, experts, N_EXPERTS), reverse=True)
    after = jnp.concatenate([at_or_after[1:], jnp.full((1,), N_EXPERTS, jnp.int32)])
    next_e = jnp.where(after < N_EXPERTS, after, -1)[block_e]
    slot = (jnp.cumsum(block_first) - 1) & 1
    rows_before = (blocks - (bend - nblk)[block_e]) * BLK_E
    block_rows = jnp.clip(counts[block_e] - rows_before, 0, BLK_E)
    return _Plan(pstart, bend, nvalid, block_e, block_first, next_e.astype(jnp.int32),
                 slot.astype(jnp.int32), block_rows.astype(jnp.int32))


def _layer(layer, x, mem, w_in, sgu_g, sgu_b, w_sp, b_sp, grp_g, w_out, wq, wkv, wo,
           w_router, b_router, w_gu, b_gu, w_down, b_down, ln_g, ln_b):
    B, S, D = x.shape
    N = B * S
    row = lambda a: a.reshape(1, -1)

    ya, q, k, v = _mixer_in(x, w_in.astype(BF16), row(sgu_g), row(sgu_b), w_sp,
                            b_sp.reshape(SGU_GROUPS, CHUNK, 1), row(grp_g[:SGU_WIDTH]))
    yb = _sb_attn(q, k, v, row(grp_g[SGU_WIDTH:]))
    k_mem, v_mem = _kv_proj(mem, wkv.astype(BF16))
    x2, x2p, idx, rank, gates, counts = _post_mixer(
        x, ya, yb, w_out.astype(BF16), ln_g, ln_b, wq.astype(BF16), k_mem, v_mem,
        wo.astype(BF16), w_router, row(b_router))

    nb = (N * TOP_K) // BLK_E + N_EXPERTS
    plan = _routing_plan(counts, nb)
    dest = _dest_rows(plan.pstart, idx.reshape(-1), rank.reshape(-1))
    dest_by_choice = dest.reshape(N, TOP_K).T
    rows = _dispatch(dest_by_choice, x2p.reshape(N, D // 2), nb * BLK_E)
    y = _experts(layer, plan, rows, w_gu, b_gu, w_down, b_down)
    y_by_choice = _gather_expert_rows(dest_by_choice, y)
    out = _combine(gates.reshape(N, TOP_K), x2.reshape(N, D), ln_g, ln_b, y_by_choice)
    return out.reshape(B, S, D)


def kernel(x, mem, w_in, sgu_g, sgu_b, w_sp, b_sp, grp_g, w_out, wq_mem, wkv_mem, wo_mem,
           w_router, b_router, w_gu, b_gu, w_down, b_down, ln_g, ln_b):
    for l in range(w_in.shape[0]):
        x = _layer(l, x, mem, w_in[l], sgu_g[l], sgu_b[l], w_sp[l], b_sp[l], grp_g[l], w_out[l],
                   wq_mem[l], wkv_mem[l], wo_mem[l], w_router[l], b_router[l], w_gu, b_gu,
                   w_down, b_down, ln_g[l], ln_b[l])
    return x
```

```python
import functools
from typing import NamedTuple

import jax
import jax.numpy as jnp
from jax import lax
from jax.experimental import pallas as pl
from jax.experimental.pallas import tpu as pltpu
from jax.experimental.pallas import tpu_sc as plsc

F32 = jnp.float32
BF16 = jnp.bfloat16

D_MODEL = 1024
SGU_WIDTH = 512
SGU_GROUPS = 4
GROUP_DIM = 128
CHUNK = 128
SB_WIDTH = 512
SB_HEADS = 8
SB_HEAD_DIM = 64
MEM_HEADS = 4
MEM_HEAD_DIM = 256
N_EXPERTS = 32
TOP_K = 4
SWIGLU_LIMIT = 7.0
SWIGLU_ALPHA = 1.702
LN_EPS = 1e-5
RMS_EPS = 1e-6
LOG2E = 1.4426950408889634

LANES = 128
TM_MIX = 1024
TQ_SB = 512
TM_POST = 1024
TD_ROWS = 256
BLK_E = 256
SC_CORES = 2
SC_SUBCORES = 16
SC_CHUNK = 64
VMEM_LIMIT = 56 * 1024 * 1024


def _layer_norm(x, g, b):
    mu = jnp.mean(x, axis=-1, keepdims=True)
    xc = x - mu
    var = jnp.mean(xc * xc, axis=-1, keepdims=True)
    return xc * lax.rsqrt(var + LN_EPS) * g + b


def _dot(a, b):
    return jnp.dot(a, b, preferred_element_type=F32)


def _dot_nt(a, b):
    return lax.dot_general(a, b, (((1,), (1,)), ((), ())), preferred_element_type=F32)


def _mixer_in_kernel(x_ref, win_ref, sg_ref, sb_ref, wsp_ref, bsp_ref, gg_ref,
                     ya_ref, q_ref, k_ref, v_ref):
    xb = x_ref[...].astype(BF16)

    def proj(lo, hi):
        return _dot(xb, win_ref[:, lo:hi])

    u = jax.nn.gelu(proj(0, SGU_WIDTH))
    vs = jax.nn.gelu(proj(SGU_WIDTH, 2 * SGU_WIDTH))
    r = lax.broadcasted_iota(jnp.int32, (CHUNK, CHUNK), 0)
    c = lax.broadcasted_iota(jnp.int32, (CHUNK, CHUNK), 1)
    causal = r >= c
    for g in range(SGU_GROUPS):
        gs = slice(g * GROUP_DIM, (g + 1) * GROUP_DIM)
        vn = _layer_norm(vs[:, gs], sg_ref[:, gs], sb_ref[:, gs]).astype(BF16)
        wc = jnp.where(causal, wsp_ref[g], 0.0).astype(BF16)
        for ch in range(TM_MIX // CHUNK):
            cs = slice(ch * CHUNK, (ch + 1) * CHUNK)
            gate = _dot(wc, vn[cs]) + bsp_ref[g]
            oa = u[cs, gs] * gate
            ms = jnp.mean(oa * oa, axis=-1, keepdims=True)
            ya_ref[cs, gs] = (oa * lax.rsqrt(ms + RMS_EPS) * gg_ref[:, gs]).astype(BF16)
    o = 2 * SGU_WIDTH
    q_ref[...] = (proj(o, o + SB_WIDTH) * (SB_HEAD_DIM ** -0.5)).astype(BF16)
    k_ref[...] = proj(o + SB_WIDTH, o + 2 * SB_WIDTH).astype(BF16)
    v_ref[...] = proj(o + 2 * SB_WIDTH, o + 3 * SB_WIDTH).astype(BF16)


def _mixer_in(x, w_in, sgu_g, sgu_b, w_sp, b_sp, grp_g_a):
    B, S, D = x.shape
    full = lambda shape: pl.BlockSpec(shape, lambda b, i: (0,) * len(shape))
    tile = lambda w: pl.BlockSpec((None, TM_MIX, w), lambda b, i: (b, i, 0))
    out = jax.ShapeDtypeStruct((B, S, SGU_WIDTH), BF16)
    return pl.pallas_call(
        _mixer_in_kernel,
        out_shape=(out, out, out, out),
        grid=(B, S // TM_MIX),
        in_specs=[tile(D), full(w_in.shape), full(sgu_g.shape), full(sgu_b.shape),
                  full(w_sp.shape), full(b_sp.shape), full(grp_g_a.shape)],
        out_specs=(tile(SGU_WIDTH),) * 4,
        compiler_params=pltpu.CompilerParams(
            dimension_semantics=("parallel", "parallel"), vmem_limit_bytes=VMEM_LIMIT),
        name="mixer_in",
    )(x, w_in, sgu_g, sgu_b, w_sp, b_sp, grp_g_a)


def _sb_attn_kernel(q_ref, k_ref, v_ref, gg_ref, o_ref):
    i = pl.program_id(2)
    lo = i * TQ_SB
    lane = lax.broadcasted_iota(jnp.int32, (1, LANES), 1)
    head0 = lane < SB_HEAD_DIM
    q_all = q_ref[...]
    r = lax.broadcasted_iota(jnp.int32, (2 * LANES, 2 * LANES), 0)
    c = lax.broadcasted_iota(jnp.int32, (2 * LANES, 2 * LANES), 1)
    after = jnp.where((r > c) & ((r // LANES) == (c // LANES)), 1.0, 0.0).astype(BF16)

    def split_heads(x):
        zx = jnp.zeros_like(x)
        return jnp.concatenate([jnp.where(head0, x, zx), jnp.where(head0, zx, x)], axis=0)

    earlier = ((lax.broadcasted_iota(jnp.int32, (LANES, 2 * LANES), 1) & (LANES - 1))
               < lax.broadcasted_iota(jnp.int32, (LANES, 2 * LANES), 0))

    def keep_earlier_keys(x):
        top = jnp.where(earlier, x[:LANES], 0.0)
        return jnp.concatenate([top, x[LANES:]], axis=0) if x.shape[0] > LANES else top

    def add_rows(x, r0, upd):
        new = x[r0:] + upd
        return jnp.concatenate([x[:r0], new], axis=0) if r0 else new

    def key_blocks(blocks, carry, acc):
        starts = [pl.multiple_of(s0, LANES) for _, s0, _ in blocks]
        k2s = [split_heads(k_ref[pl.ds(s, LANES), :]) for s in starts]
        v2s = [split_heads(v_ref[pl.ds(s, LANES), :]) for s in starts]
        zs = [_dot_nt(q_all[r0:], k2) for (r0, _, _), k2 in zip(blocks, k2s)]
        sps = []
        for (_, _, causal), z in zip(blocks, zs):
            sp = jnp.maximum(z, 0.0) + jnp.log(1.0 + jnp.exp2(jnp.abs(z) * (-LOG2E)))
            if causal:
                sp = keep_earlier_keys(sp)
            sps.append(sp)
        his = [sp.astype(BF16) for sp in sps]
        los = [(sp - hi.astype(F32)).astype(BF16) for sp, hi in zip(sps, his)]
        laters = [_dot(hi, after) + _dot(lo_part, after) for hi, lo_part in zip(his, los)]
        weights = []
        for (r0, _, causal), z, sp, later in zip(blocks, zs, sps, laters):
            a = jnp.exp(z - (sp + later + carry[r0:]))
            if causal:
                a = keep_earlier_keys(a)
            weights.append(a.astype(BF16))
            rows = [jnp.broadcast_to(jnp.sum(sp[:, h * LANES:(h + 1) * LANES], axis=1, keepdims=True),
                                     (z.shape[0], LANES)) for h in range(2)]
            carry = add_rows(carry, r0, jnp.concatenate(rows, axis=1))
        for (r0, _, _), a, v2 in zip(blocks, weights, v2s):
            acc = add_rows(acc, r0, _dot(a, v2))
        return carry, acc

    nsub = TQ_SB // LANES
    carry = jnp.zeros((TQ_SB, 2 * LANES), F32)
    acc = jnp.zeros((TQ_SB, LANES), F32)
    carry, acc = key_blocks([(sb * LANES, lo + sb * LANES, True) for sb in reversed(range(nsub))],
                            carry, acc)

    def body(j, st):
        s0 = pl.multiple_of(lo - (j + 1) * TQ_SB, TQ_SB)
        return key_blocks([(0, s0 + sb * LANES, False) for sb in reversed(range(nsub))], *st)

    carry, acc = lax.fori_loop(0, i, body, (carry, acc))

    sq = acc * acc
    ss0 = jnp.sum(jnp.where(head0, sq, 0.0), axis=1, keepdims=True)
    ss1 = jnp.sum(jnp.where(head0, 0.0, sq), axis=1, keepdims=True)
    inv = 1.0 / SB_HEAD_DIM
    rs = jnp.where(head0, lax.rsqrt(ss0 * inv + RMS_EPS), lax.rsqrt(ss1 * inv + RMS_EPS))
    o_ref[...] = (acc * rs * gg_ref[...]).astype(BF16)


def _sb_attn(q, k, v, grp_g_b):
    B, S, W = q.shape
    pairs = W // LANES
    return pl.pallas_call(
        _sb_attn_kernel,
        out_shape=jax.ShapeDtypeStruct((B, S, W), BF16),
        grid=(B, pairs, S // TQ_SB),
        in_specs=[pl.BlockSpec((None, TQ_SB, LANES), lambda b, p, i: (b, i, p)),
                  pl.BlockSpec((None, S, LANES), lambda b, p, i: (b, 0, p)),
                  pl.BlockSpec((None, S, LANES), lambda b, p, i: (b, 0, p)),
                  pl.BlockSpec((1, LANES), lambda b, p, i: (0, p))],
        out_specs=pl.BlockSpec((None, TQ_SB, LANES), lambda b, p, i: (b, i, p)),
        compiler_params=pltpu.CompilerParams(
            dimension_semantics=("parallel", "parallel", "arbitrary"),
            vmem_limit_bytes=VMEM_LIMIT),
        name="sb_attn",
    )(q, k, v, grp_g_b)


def _kv_proj_kernel(mem_ref, wkv_ref, k_ref, v_ref):
    mb = mem_ref[...].astype(BF16)
    k_ref[...] = _dot(mb, wkv_ref[:, :D_MODEL]).astype(BF16)
    v_ref[...] = _dot(mb, wkv_ref[:, D_MODEL:]).astype(BF16)


def _kv_proj(mem, wkv):
    B, M, D = mem.shape
    out = jax.ShapeDtypeStruct((B, M, D), BF16)
    return pl.pallas_call(
        _kv_proj_kernel,
        out_shape=(out, out),
        grid=(B,),
        in_specs=[pl.BlockSpec((None, M, D), lambda b: (b, 0, 0)),
                  pl.BlockSpec(wkv.shape, lambda b: (0, 0))],
        out_specs=(pl.BlockSpec((None, M, D), lambda b: (b, 0, 0)),) * 2,
        compiler_params=pltpu.CompilerParams(
            dimension_semantics=("parallel",), vmem_limit_bytes=VMEM_LIMIT),
        name="kv_proj",
    )(mem, wkv)


def _pack_bf16_pairs(x):
    w = x.shape[1] // 2
    bits = lax.bitcast_convert_type(x.astype(BF16).astype(F32), jnp.uint32)
    return (bits[:, w:] & jnp.uint32(0xFFFF0000)) | (bits[:, :w] >> 16)


def _unpack_bf16_pairs(u):
    lo = lax.bitcast_convert_type(u << 16, F32).astype(BF16)
    hi = lax.bitcast_convert_type(u & jnp.uint32(0xFFFF0000), F32).astype(BF16)
    return lo, hi


def _post_mixer_kernel(x_ref, ya_ref, yb_ref, wout_ref, lng_ref, lnb_ref, wq_ref, km_ref, vm_ref,
                       wo_ref, wr_ref, br_ref,
                       x2_ref, x2p_ref, idx_ref, rank_ref, gate_ref, cnt_ref, carry_ref):
    first = jnp.logical_and(pl.program_id(0) == 0, pl.program_id(1) == 0)

    @pl.when(first)
    def _():
        carry_ref[...] = jnp.zeros_like(carry_ref)

    alpha = (2 * 2) ** 0.25
    mix = _dot(ya_ref[...], wout_ref[:SGU_WIDTH, :]) + _dot(yb_ref[...], wout_ref[SGU_WIDTH:, :])
    x1 = _layer_norm(alpha * x_ref[...] + mix, lng_ref[0:1, :], lnb_ref[0:1, :])

    qm = (_dot(x1.astype(BF16), wq_ref[...]) * (MEM_HEAD_DIM ** -0.5)).astype(BF16)
    heads = []
    for h in range(MEM_HEADS):
        hs = slice(h * MEM_HEAD_DIM, (h + 1) * MEM_HEAD_DIM)
        s = _dot_nt(qm[:, hs], km_ref[:, hs])
        e = jnp.exp(s - jnp.max(s, axis=-1, keepdims=True))
        p = (e / jnp.sum(e, axis=-1, keepdims=True)).astype(BF16)
        heads.append(_dot(p, vm_ref[:, hs]).astype(BF16))
    o = jnp.concatenate(heads, axis=1)
    x2 = _layer_norm(alpha * x1 + _dot(o, wo_ref[...]), lng_ref[1:2, :], lnb_ref[1:2, :])
    x2_ref[...] = x2
    x2p_ref[...] = _pack_bf16_pairs(x2)

    xh = x2.astype(BF16)
    xl = (x2 - xh.astype(F32)).astype(BF16)
    wr = wr_ref[...]
    wh = wr.astype(BF16)
    wl = (wr - wh.astype(F32)).astype(BF16)
    logits = _dot(xh, wh) + (_dot(xl, wh) + _dot(xh, wl)) + br_ref[...]

    tm = logits.shape[0]
    lane = lax.broadcasted_iota(jnp.int32, (tm, N_EXPERTS), 1).astype(F32)
    work = logits
    vals, idxs = [], []
    chosen = jnp.zeros((tm, N_EXPERTS), F32)
    for _ in range(TOP_K):
        m = jnp.max(work, axis=-1, keepdims=True)
        ix = jnp.min(jnp.where(work == m, lane, float(N_EXPERTS)), axis=-1, keepdims=True)
        sel = lane == ix
        work = jnp.where(sel, -jnp.inf, work)
        chosen = jnp.where(sel, 1.0, chosen)
        vals.append(m)
        idxs.append(ix)
    ev = [jnp.exp(v - vals[0]) for v in vals]
    den = ev[0] + ev[1] + ev[2] + ev[3]

    r = lax.broadcasted_iota(jnp.int32, (tm, tm), 0)
    c = lax.broadcasted_iota(jnp.int32, (tm, tm), 1)
    before = jnp.where(r > c, 1.0, 0.0).astype(BF16)
    running = _dot(before, chosen.astype(BF16)) + carry_ref[...]
    ranks = [jnp.sum(jnp.where(lane == ix, running, 0.0), axis=-1, keepdims=True) for ix in idxs]
    carry_ref[...] = carry_ref[...] + jnp.sum(chosen, axis=0, keepdims=True)
    cnt_ref[...] = carry_ref[...]

    l4 = lax.broadcasted_iota(jnp.int32, (tm, TOP_K), 1)

    def cols(xs):
        return jnp.where(l4 == 0, xs[0], jnp.where(l4 == 1, xs[1], jnp.where(l4 == 2, xs[2], xs[3])))

    idx_ref[...] = cols(idxs).astype(jnp.int32)
    rank_ref[...] = cols(ranks).astype(jnp.int32)
    gate_ref[...] = cols([e / den for e in ev])


def _post_mixer(x, ya, yb, w_out, ln_g, ln_b, wq, k_mem, v_mem, wo, w_router, b_router):
    B, S, D = x.shape
    M = k_mem.shape[1]
    full = lambda a: pl.BlockSpec(a.shape, lambda b, i: (0,) * a.ndim)
    tile = lambda w: pl.BlockSpec((None, TM_POST, w), lambda b, i: (b, i, 0))
    memb = pl.BlockSpec((None, M, D), lambda b, i: (b, 0, 0))
    sds = jax.ShapeDtypeStruct
    return pl.pallas_call(
        _post_mixer_kernel,
        out_shape=(sds((B, S, D), F32), sds((B, S, D // 2), jnp.uint32),
                   sds((B, S, TOP_K), jnp.int32), sds((B, S, TOP_K), jnp.int32),
                   sds((B, S, TOP_K), F32), sds((1, N_EXPERTS), F32)),
        grid=(B, S // TM_POST),
        in_specs=[tile(D), tile(SGU_WIDTH), tile(SB_WIDTH), full(w_out), full(ln_g), full(ln_b),
                  full(wq), memb, memb, full(wo), full(w_router), full(b_router)],
        out_specs=(tile(D), tile(D // 2), tile(TOP_K), tile(TOP_K), tile(TOP_K),
                   pl.BlockSpec((1, N_EXPERTS), lambda b, i: (0, 0))),
        scratch_shapes=[pltpu.VMEM((1, N_EXPERTS), F32)],
        compiler_params=pltpu.CompilerParams(
            dimension_semantics=("arbitrary", "arbitrary"), vmem_limit_bytes=VMEM_LIMIT),
        name="post_mixer",
    )(x, ya, yb, w_out, ln_g, ln_b, wq, k_mem, v_mem, wo, w_router, b_router)


def _dest_rows_kernel(pstart_ref, idx_ref, rank_ref, dest_ref):
    idx = idx_ref[...]
    base = jnp.zeros_like(idx)
    for e in range(N_EXPERTS):
        base = jnp.where(idx == e, pstart_ref[e], base)
    dest_ref[...] = base + rank_ref[...]


def _dest_rows(pstart, idx_flat, rank_flat):
    n = idx_flat.shape[0]
    shape2d = (n // LANES, LANES)
    full = pl.BlockSpec(shape2d, lambda i, ps: (0, 0))
    dest = pl.pallas_call(
        _dest_rows_kernel,
        out_shape=jax.ShapeDtypeStruct(shape2d, jnp.int32),
        grid_spec=pltpu.PrefetchScalarGridSpec(
            num_scalar_prefetch=1, grid=(1,), in_specs=[full, full], out_specs=full),
        name="dest_rows",
    )(pstart, idx_flat.reshape(shape2d), rank_flat.reshape(shape2d))
    return dest.reshape(n)


def _dispatch(dest_by_choice, x2p, n_rows):
    N, W = x2p.shape
    workers = SC_CORES * SC_SUBCORES
    per_worker = N // workers
    mesh = plsc.VectorSubcoreMesh(core_axis_name="core", subcore_axis_name="subcore",
                                  num_cores=SC_CORES, num_subcores=SC_SUBCORES)

    @functools.partial(
        pl.kernel, mesh=mesh,
        out_type=jax.ShapeDtypeStruct((n_rows, W), jnp.uint32),
        scratch_types=[pltpu.VMEM((SC_CHUNK,), jnp.int32), pltpu.VMEM((SC_CHUNK, W), jnp.uint32)],
        name="dispatch",
    )
    def scatter(x_hbm, dest_hbm, rows_hbm, idx_v, rows_v):
        worker = lax.axis_index("subcore") * SC_CORES + lax.axis_index("core")

        @pl.loop(0, per_worker // SC_CHUNK)
        def _(c):
            base = worker * per_worker + c * SC_CHUNK
            pltpu.sync_copy(x_hbm.at[pl.ds(base, SC_CHUNK)], rows_v)
            for k in range(TOP_K):
                pltpu.sync_copy(dest_hbm.at[k, pl.ds(base, SC_CHUNK)], idx_v)
                pltpu.sync_copy(rows_v, rows_hbm.at[idx_v])

    return scatter(x2p, dest_by_choice)


def _experts_kernel(layer, be_ref, first_ref, nvalid_ref, next_ref, slot_ref, used_ref,
                    rows_ref, bgu_ref, bd_ref, wgu_hbm, wd_hbm,
                    y_ref, wgu_f32, wd_f32, wgu_bf, wd_bf, sem):
    i = pl.program_id(0)

    def fetch(e, slot):
        return (pltpu.make_async_copy(wgu_hbm.at[layer, e], wgu_f32.at[slot], sem.at[0, slot]),
                pltpu.make_async_copy(wd_hbm.at[layer, e], wd_f32.at[slot], sem.at[1, slot]))

    @pl.when(i < nvalid_ref[0])
    def _():
        @pl.when(first_ref[i] == 1)
        def _():
            slot = slot_ref[i]

            @pl.when(i == 0)
            def _():
                for c in fetch(be_ref[i], slot):
                    c.start()

            for c in fetch(be_ref[i], slot):
                c.wait()
            wgu_bf[...] = wgu_f32[slot].astype(BF16)
            wd_bf[...] = wd_f32[slot].astype(BF16)

            @pl.when(next_ref[i] >= 0)
            def _():
                for c in fetch(next_ref[i], 1 - slot):
                    c.start()

        live = lax.broadcasted_iota(jnp.int32, rows_ref.shape, 0) < used_ref[i]
        lo, hi = _unpack_bf16_pairs(jnp.where(live, rows_ref[...], jnp.uint32(0)))
        half = D_MODEL // 2
        hgu = _dot(lo, wgu_bf[:half, :]) + _dot(hi, wgu_bf[half:, :]) + bgu_ref[...]
        g = jnp.minimum(hgu[:, :D_MODEL], SWIGLU_LIMIT)
        u = jnp.clip(hgu[:, D_MODEL:], -SWIGLU_LIMIT, SWIGLU_LIMIT)
        a = (u + 1.0) * (g * jax.nn.sigmoid(SWIGLU_ALPHA * g))
        y_ref[...] = _pack_bf16_pairs(_dot(a.astype(BF16), wd_bf[...]) + bd_ref[...])

    @pl.when(i >= nvalid_ref[0])
    def _():
        y_ref[...] = jnp.zeros_like(y_ref)


def _experts(layer, plan, rows, w_gu, b_gu, w_down, b_down):
    P, W = rows.shape
    L, E, D, F2 = w_gu.shape
    nb = P // BLK_E

    def row_map(i, be, bf, nv, nx, sl, us):
        return (jnp.minimum(i, nv[0] - 1), 0)

    def b_map(i, be, bf, nv, nx, sl, us):
        return (layer, be[i], 0, 0)

    grid_spec = pltpu.PrefetchScalarGridSpec(
        num_scalar_prefetch=6,
        grid=(nb,),
        in_specs=[pl.BlockSpec((BLK_E, W), row_map),
                  pl.BlockSpec((None, None, 1, F2), b_map),
                  pl.BlockSpec((None, None, 1, D), b_map),
                  pl.BlockSpec(memory_space=pl.ANY),
                  pl.BlockSpec(memory_space=pl.ANY)],
        out_specs=pl.BlockSpec((BLK_E, D // 2), lambda i, *_: (i, 0)),
        scratch_shapes=[pltpu.VMEM((2, D, F2), F32), pltpu.VMEM((2, F2 // 2, D), F32),
                        pltpu.VMEM((D, F2), BF16), pltpu.VMEM((F2 // 2, D), BF16),
                        pltpu.SemaphoreType.DMA((2, 2))],
    )
    return pl.pallas_call(
        functools.partial(_experts_kernel, layer),
        out_shape=jax.ShapeDtypeStruct((P, D // 2), jnp.uint32),
        grid_spec=grid_spec,
        compiler_params=pltpu.CompilerParams(
            dimension_semantics=("arbitrary",), vmem_limit_bytes=VMEM_LIMIT),
        name="experts",
    )(plan.block_e, plan.block_first, plan.nvalid, plan.next_e, plan.slot, plan.block_rows, rows,
      b_gu.reshape(L, E, 1, F2), b_down.reshape(L, E, 1, D), w_gu, w_down)


def _gather_expert_rows(dest_by_choice, y):
    n_choice, N = dest_by_choice.shape
    D = y.shape[1]
    workers = SC_CORES * SC_SUBCORES
    per_worker = N // workers
    mesh = plsc.VectorSubcoreMesh(core_axis_name="core", subcore_axis_name="subcore",
                                  num_cores=SC_CORES, num_subcores=SC_SUBCORES)

    @functools.partial(
        pl.kernel, mesh=mesh,
        out_type=jax.ShapeDtypeStruct((n_choice, N, D), y.dtype),
        scratch_types=[pltpu.VMEM((SC_CHUNK,), jnp.int32), pltpu.VMEM((SC_CHUNK, D), y.dtype)],
        name="gather_expert_rows",
    )
    def gather(y_hbm, dest_hbm, out_hbm, idx_v, rows_v):
        worker = lax.axis_index("subcore") * SC_CORES + lax.axis_index("core")

        @pl.loop(0, per_worker // SC_CHUNK)
        def _(c):
            base = worker * per_worker + c * SC_CHUNK
            for k in range(n_choice):
                pltpu.sync_copy(dest_hbm.at[k, pl.ds(base, SC_CHUNK)], idx_v)
                pltpu.sync_copy(y_hbm.at[idx_v], rows_v)
                pltpu.sync_copy(rows_v, out_hbm.at[k, pl.ds(base, SC_CHUNK)])

    return gather(y, dest_by_choice)


def _combine_kernel(gate_ref, x2_ref, lng_ref, lnb_ref, yk_ref, o_ref):
    gates = gate_ref[...]
    halves = None
    for k in range(TOP_K):
        u = yk_ref[k]
        lo = lax.bitcast_convert_type(u << 16, F32)
        hi = lax.bitcast_convert_type(u & jnp.uint32(0xFFFF0000), F32)
        g = gates[:, k:k + 1]
        halves = (g * lo, g * hi) if halves is None else (halves[0] + g * lo, halves[1] + g * hi)
    ff = jnp.concatenate(halves, axis=1)
    alpha = (2 * 2) ** 0.25
    o_ref[...] = _layer_norm(alpha * x2_ref[...] + ff, lng_ref[2:3, :], lnb_ref[2:3, :])


def _combine(gates, x2, ln_g, ln_b, y_by_choice):
    N, D = x2.shape
    return pl.pallas_call(
        _combine_kernel,
        out_shape=jax.ShapeDtypeStruct((N, D), F32),
        grid=(N // TD_ROWS,),
        in_specs=[pl.BlockSpec((TD_ROWS, TOP_K), lambda i: (i, 0)),
                  pl.BlockSpec((TD_ROWS, D), lambda i: (i, 0)),
                  pl.BlockSpec(ln_g.shape, lambda i: (0, 0)),
                  pl.BlockSpec(ln_b.shape, lambda i: (0, 0)),
                  pl.BlockSpec((TOP_K, TD_ROWS, D // 2), lambda i: (0, i, 0))],
        out_specs=pl.BlockSpec((TD_ROWS, D), lambda i: (i, 0)),
        compiler_params=pltpu.CompilerParams(
            dimension_semantics=("parallel",), vmem_limit_bytes=VMEM_LIMIT),
        name="combine",
    )(gates, x2, ln_g, ln_b, y_by_choice)


class _Plan(NamedTuple):
    pstart: jax.Array
    bend: jax.Array
    nvalid: jax.Array
    block_e: jax.Array
    block_first: jax.Array
    next_e: jax.Array
    slot: jax.Array
    block_rows: jax.Array


def _routing_plan(counts, nb):
    counts = counts.reshape(N_EXPERTS).astype(jnp.int32)
    nblk = (counts + BLK_E - 1) // BLK_E
    bend = jnp.cumsum(nblk)
    pstart = (bend - nblk) * BLK_E
    nvalid = bend[-1:]
    blocks = jnp.minimum(jnp.arange(nb, dtype=jnp.int32), nvalid[0] - 1)
    block_e = jnp.sum((blocks[:, None] >= bend[None, :]).astype(jnp.int32), axis=1)
    block_e = jnp.minimum(block_e, N_EXPERTS - 1)
    block_first = jnp.concatenate(
        [jnp.ones((1,), jnp.int32), (block_e[1:] != block_e[:-1]).astype(jnp.int32)])
    experts = jnp.arange(N_EXPERTS, dtype=jnp.int32)
    at_or_after = lax.cummin(jnp.where(nblk > 0, experts, N_EXPERTS), reverse=True)
    after = jnp.concatenate([at_or_after[1:], jnp.full((1,), N_EXPERTS, jnp.int32)])
    next_e = jnp.where(after < N_EXPERTS, after, -1)[block_e]
    slot = (jnp.cumsum(block_first) - 1) & 1
    rows_before = (blocks - (bend - nblk)[block_e]) * BLK_E
    block_rows = jnp.clip(counts[block_e] - rows_before, 0, BLK_E)
    return _Plan(pstart, bend, nvalid, block_e, block_first, next_e.astype(jnp.int32),
                 slot.astype(jnp.int32), block_rows.astype(jnp.int32))


def _layer(layer, x, mem, w_in, sgu_g, sgu_b, w_sp, b_sp, grp_g, w_out, wq, wkv, wo,
           w_router, b_router, w_gu, b_gu, w_down, b_down, ln_g, ln_b):
    B, S, D = x.shape
    N = B * S
    row = lambda a: a.reshape(1, -1)

    ya, q, k, v = _mixer_in(x, w_in.astype(BF16), row(sgu_g), row(sgu_b), w_sp,
                            b_sp.reshape(SGU_GROUPS, CHUNK, 1), row(grp_g[:SGU_WIDTH]))
    yb = _sb_attn(q, k, v, row(grp_g[SGU_WIDTH:]))
    k_mem, v_mem = _kv_proj(mem, wkv.astype(BF16))
    x2, x2p, idx, rank, gates, counts = _post_mixer(
        x, ya, yb, w_out.astype(BF16), ln_g, ln_b, wq.astype(BF16), k_mem, v_mem,
        wo.astype(BF16), w_router, row(b_router))

    nb = (N * TOP_K) // BLK_E + N_EXPERTS
    plan = _routing_plan(counts, nb)
    dest = _dest_rows(plan.pstart, idx.reshape(-1), rank.reshape(-1))
    dest_by_choice = dest.reshape(N, TOP_K).T
    rows = _dispatch(dest_by_choice, x2p.reshape(N, D // 2), nb * BLK_E)
    y = _experts(layer, plan, rows, w_gu, b_gu, w_down, b_down)
    y_by_choice = _gather_expert_rows(dest_by_choice, y)
    out = _combine(gates.reshape(N, TOP_K), x2.reshape(N, D), ln_g, ln_b, y_by_choice)
    return out.reshape(B, S, D)


def kernel(x, mem, w_in, sgu_g, sgu_b, w_sp, b_sp, grp_g, w_out, wq_mem, wkv_mem, wo_mem,
           w_router, b_router, w_gu, b_gu, w_down, b_down, ln_g, ln_b):
    for l in range(w_in.shape[0]):
        x = _layer(l, x, mem, w_in[l], sgu_g[l], sgu_b[l], w_sp[l], b_sp[l], grp_g[l], w_out[l],
                   wq_mem[l], wkv_mem[l], wo_mem[l], w_router[l], b_router[l], w_gu, b_gu,
                   w_down, b_down, ln_g[l], ln_b[l])
    return x
```

```python
import functools
from typing import NamedTuple

import jax
import jax.numpy as jnp
from jax import lax
from jax.experimental import pallas as pl
from jax.experimental.pallas import tpu as pltpu
from jax.experimental.pallas import tpu_sc as plsc

F32 = jnp.float32
BF16 = jnp.bfloat16

D_MODEL = 1024
SGU_WIDTH = 512
SGU_GROUPS = 4
GROUP_DIM = 128
CHUNK = 128
SB_WIDTH = 512
SB_HEADS = 8
SB_HEAD_DIM = 64
MEM_HEADS = 4
MEM_HEAD_DIM = 256
N_EXPERTS = 32
TOP_K = 4
SWIGLU_LIMIT = 7.0
SWIGLU_ALPHA = 1.702
LN_EPS = 1e-5
RMS_EPS = 1e-6
LOG2E = 1.4426950408889634

LANES = 128
TM_MIX = 1024
TQ_SB = 512
TM_POST = 1024
TD_ROWS = 256
BLK_E = 256
SC_CORES = 2
SC_SUBCORES = 16
SC_CHUNK = 128
VMEM_LIMIT = 56 * 1024 * 1024


def _layer_norm(x, g, b):
    mu = jnp.mean(x, axis=-1, keepdims=True)
    xc = x - mu
    var = jnp.mean(xc * xc, axis=-1, keepdims=True)
    return xc * lax.rsqrt(var + LN_EPS) * g + b


def _dot(a, b):
    return jnp.dot(a, b, preferred_element_type=F32)


def _dot_nt(a, b):
    return lax.dot_general(a, b, (((1,), (1,)), ((), ())), preferred_element_type=F32)


def _mixer_in_kernel(x_ref, win_ref, sg_ref, sb_ref, wsp_ref, bsp_ref, gg_ref,
                     ya_ref, q_ref, k_ref, v_ref):
    xb = x_ref[...].astype(BF16)

    def proj(lo, hi):
        return _dot(xb, win_ref[:, lo:hi])

    u = jax.nn.gelu(proj(0, SGU_WIDTH))
    vs = jax.nn.gelu(proj(SGU_WIDTH, 2 * SGU_WIDTH))
    r = lax.broadcasted_iota(jnp.int32, (CHUNK, CHUNK), 0)
    c = lax.broadcasted_iota(jnp.int32, (CHUNK, CHUNK), 1)
    causal = r >= c
    for g in range(SGU_GROUPS):
        gs = slice(g * GROUP_DIM, (g + 1) * GROUP_DIM)
        vn = _layer_norm(vs[:, gs], sg_ref[:, gs], sb_ref[:, gs]).astype(BF16)
        wc = jnp.where(causal, wsp_ref[g], 0.0).astype(BF16)
        for ch in range(TM_MIX // CHUNK):
            cs = slice(ch * CHUNK, (ch + 1) * CHUNK)
            gate = _dot(wc, vn[cs]) + bsp_ref[g]
            oa = u[cs, gs] * gate
            ms = jnp.mean(oa * oa, axis=-1, keepdims=True)
            ya_ref[cs, gs] = (oa * lax.rsqrt(ms + RMS_EPS) * gg_ref[:, gs]).astype(BF16)
    o = 2 * SGU_WIDTH
    q_ref[...] = (proj(o, o + SB_WIDTH) * (SB_HEAD_DIM ** -0.5)).astype(BF16)
    k_ref[...] = proj(o + SB_WIDTH, o + 2 * SB_WIDTH).astype(BF16)
    v_ref[...] = proj(o + 2 * SB_WIDTH, o + 3 * SB_WIDTH).astype(BF16)


def _mixer_in(x, w_in, sgu_g, sgu_b, w_sp, b_sp, grp_g_a):
    B, S, D = x.shape
    full = lambda shape: pl.BlockSpec(shape, lambda b, i: (0,) * len(shape))
    tile = lambda w: pl.BlockSpec((None, TM_MIX, w), lambda b, i: (b, i, 0))
    out = jax.ShapeDtypeStruct((B, S, SGU_WIDTH), BF16)
    return pl.pallas_call(
        _mixer_in_kernel,
        out_shape=(out, out, out, out),
        grid=(B, S // TM_MIX),
        in_specs=[tile(D), full(w_in.shape), full(sgu_g.shape), full(sgu_b.shape),
                  full(w_sp.shape), full(b_sp.shape), full(grp_g_a.shape)],
        out_specs=(tile(SGU_WIDTH),) * 4,
        compiler_params=pltpu.CompilerParams(
            dimension_semantics=("parallel", "parallel"), vmem_limit_bytes=VMEM_LIMIT),
        name="mixer_in",
    )(x, w_in, sgu_g, sgu_b, w_sp, b_sp, grp_g_a)


def _sb_attn_kernel(q_ref, k_ref, v_ref, gg_ref, o_ref):
    i = pl.program_id(2)
    lo = i * TQ_SB
    lane = lax.broadcasted_iota(jnp.int32, (1, LANES), 1)
    head0 = lane < SB_HEAD_DIM
    q_all = q_ref[...]
    r = lax.broadcasted_iota(jnp.int32, (2 * LANES, 2 * LANES), 0)
    c = lax.broadcasted_iota(jnp.int32, (2 * LANES, 2 * LANES), 1)
    after = jnp.where((r > c) & ((r // LANES) == (c // LANES)), 1.0, 0.0).astype(BF16)

    def split_heads(x):
        zx = jnp.zeros_like(x)
        return jnp.concatenate([jnp.where(head0, x, zx), jnp.where(head0, zx, x)], axis=0)

    earlier = ((lax.broadcasted_iota(jnp.int32, (LANES, 2 * LANES), 1) & (LANES - 1))
               < lax.broadcasted_iota(jnp.int32, (LANES, 2 * LANES), 0))

    def keep_earlier_keys(x):
        top = jnp.where(earlier, x[:LANES], 0.0)
        return jnp.concatenate([top, x[LANES:]], axis=0) if x.shape[0] > LANES else top

    def add_rows(x, r0, upd):
        new = x[r0:] + upd
        return jnp.concatenate([x[:r0], new], axis=0) if r0 else new

    def key_blocks(blocks, carry, acc):
        starts = [pl.multiple_of(s0, LANES) for _, s0, _ in blocks]
        k2s = [split_heads(k_ref[pl.ds(s, LANES), :]) for s in starts]
        v2s = [split_heads(v_ref[pl.ds(s, LANES), :]) for s in starts]
        zs = [_dot_nt(q_all[r0:], k2) for (r0, _, _), k2 in zip(blocks, k2s)]
        sps = []
        for (_, _, causal), z in zip(blocks, zs):
            sp = jnp.maximum(z, 0.0) + jnp.log(1.0 + jnp.exp2(jnp.abs(z) * (-LOG2E)))
            if causal:
                sp = keep_earlier_keys(sp)
            sps.append(sp)
        his = [sp.astype(BF16) for sp in sps]
        los = [(sp - hi.astype(F32)).astype(BF16) for sp, hi in zip(sps, his)]
        laters = [_dot(hi, after) + _dot(lo_part, after) for hi, lo_part in zip(his, los)]
        weights = []
        for (r0, _, causal), z, sp, later in zip(blocks, zs, sps, laters):
            a = jnp.exp(z - (sp + later + carry[r0:]))
            if causal:
                a = keep_earlier_keys(a)
            weights.append(a.astype(BF16))
            rows = [jnp.broadcast_to(jnp.sum(sp[:, h * LANES:(h + 1) * LANES], axis=1, keepdims=True),
                                     (z.shape[0], LANES)) for h in range(2)]
            carry = add_rows(carry, r0, jnp.concatenate(rows, axis=1))
        for (r0, _, _), a, v2 in zip(blocks, weights, v2s):
            acc = add_rows(acc, r0, _dot(a, v2))
        return carry, acc

    nsub = TQ_SB // LANES
    carry = jnp.zeros((TQ_SB, 2 * LANES), F32)
    acc = jnp.zeros((TQ_SB, LANES), F32)
    carry, acc = key_blocks([(sb * LANES, lo + sb * LANES, True) for sb in reversed(range(nsub))],
                            carry, acc)

    def body(j, st):
        s0 = pl.multiple_of(lo - (j + 1) * TQ_SB, TQ_SB)
        return key_blocks([(0, s0 + sb * LANES, False) for sb in reversed(range(nsub))], *st)

    carry, acc = lax.fori_loop(0, i, body, (carry, acc))

    sq = acc * acc
    ss0 = jnp.sum(jnp.where(head0, sq, 0.0), axis=1, keepdims=True)
    ss1 = jnp.sum(jnp.where(head0, 0.0, sq), axis=1, keepdims=True)
    inv = 1.0 / SB_HEAD_DIM
    rs = jnp.where(head0, lax.rsqrt(ss0 * inv + RMS_EPS), lax.rsqrt(ss1 * inv + RMS_EPS))
    o_ref[...] = (acc * rs * gg_ref[...]).astype(BF16)


def _sb_attn(q, k, v, grp_g_b):
    B, S, W = q.shape
    pairs = W // LANES
    return pl.pallas_call(
        _sb_attn_kernel,
        out_shape=jax.ShapeDtypeStruct((B, S, W), BF16),
        grid=(B, pairs, S // TQ_SB),
        in_specs=[pl.BlockSpec((None, TQ_SB, LANES), lambda b, p, i: (b, i, p)),
                  pl.BlockSpec((None, S, LANES), lambda b, p, i: (b, 0, p)),
                  pl.BlockSpec((None, S, LANES), lambda b, p, i: (b, 0, p)),
                  pl.BlockSpec((1, LANES), lambda b, p, i: (0, p))],
        out_specs=pl.BlockSpec((None, TQ_SB, LANES), lambda b, p, i: (b, i, p)),
        compiler_params=pltpu.CompilerParams(
            dimension_semantics=("parallel", "parallel", "arbitrary"),
            vmem_limit_bytes=VMEM_LIMIT),
        name="sb_attn",
    )(q, k, v, grp_g_b)


def _kv_proj_kernel(mem_ref, wkv_ref, k_ref, v_ref):
    mb = mem_ref[...].astype(BF16)
    k_ref[...] = _dot(mb, wkv_ref[:, :D_MODEL]).astype(BF16)
    v_ref[...] = _dot(mb, wkv_ref[:, D_MODEL:]).astype(BF16)


def _kv_proj(mem, wkv):
    B, M, D = mem.shape
    out = jax.ShapeDtypeStruct((B, M, D), BF16)
    return pl.pallas_call(
        _kv_proj_kernel,
        out_shape=(out, out),
        grid=(B,),
        in_specs=[pl.BlockSpec((None, M, D), lambda b: (b, 0, 0)),
                  pl.BlockSpec(wkv.shape, lambda b: (0, 0))],
        out_specs=(pl.BlockSpec((None, M, D), lambda b: (b, 0, 0)),) * 2,
        compiler_params=pltpu.CompilerParams(
            dimension_semantics=("parallel",), vmem_limit_bytes=VMEM_LIMIT),
        name="kv_proj",
    )(mem, wkv)


def _pack_bf16_pairs(x):
    w = x.shape[1] // 2
    bits = lax.bitcast_convert_type(x.astype(BF16).astype(F32), jnp.uint32)
    return (bits[:, w:] & jnp.uint32(0xFFFF0000)) | (bits[:, :w] >> 16)


def _unpack_bf16_pairs(u):
    lo = lax.bitcast_convert_type(u << 16, F32).astype(BF16)
    hi = lax.bitcast_convert_type(u & jnp.uint32(0xFFFF0000), F32).astype(BF16)
    return lo, hi


def _post_mixer_kernel(x_ref, ya_ref, yb_ref, wout_ref, lng_ref, lnb_ref, wq_ref, km_ref, vm_ref,
                       wo_ref, wr_ref, br_ref,
                       x2_ref, x2p_ref, idx_ref, rank_ref, gate_ref, cnt_ref, carry_ref):
    first = jnp.logical_and(pl.program_id(0) == 0, pl.program_id(1) == 0)

    @pl.when(first)
    def _():
        carry_ref[...] = jnp.zeros_like(carry_ref)

    alpha = (2 * 2) ** 0.25
    mix = _dot(ya_ref[...], wout_ref[:SGU_WIDTH, :]) + _dot(yb_ref[...], wout_ref[SGU_WIDTH:, :])
    x1 = _layer_norm(alpha * x_ref[...] + mix, lng_ref[0:1, :], lnb_ref[0:1, :])

    qm = (_dot(x1.astype(BF16), wq_ref[...]) * (MEM_HEAD_DIM ** -0.5)).astype(BF16)
    heads = []
    for h in range(MEM_HEADS):
        hs = slice(h * MEM_HEAD_DIM, (h + 1) * MEM_HEAD_DIM)
        s = _dot_nt(qm[:, hs], km_ref[:, hs])
        e = jnp.exp(s - jnp.max(s, axis=-1, keepdims=True))
        p = (e / jnp.sum(e, axis=-1, keepdims=True)).astype(BF16)
        heads.append(_dot(p, vm_ref[:, hs]).astype(BF16))
    o = jnp.concatenate(heads, axis=1)
    x2 = _layer_norm(alpha * x1 + _dot(o, wo_ref[...]), lng_ref[1:2, :], lnb_ref[1:2, :])
    x2_ref[...] = x2
    x2p_ref[...] = _pack_bf16_pairs(x2)

    xh = x2.astype(BF16)
    xl = (x2 - xh.astype(F32)).astype(BF16)
    wr = wr_ref[...]
    wh = wr.astype(BF16)
    wl = (wr - wh.astype(F32)).astype(BF16)
    logits = _dot(xh, wh) + (_dot(xl, wh) + _dot(xh, wl)) + br_ref[...]

    tm = logits.shape[0]
    lane = lax.broadcasted_iota(jnp.int32, (tm, N_EXPERTS), 1).astype(F32)
    work = logits
    vals, idxs = [], []
    chosen = jnp.zeros((tm, N_EXPERTS), F32)
    for _ in range(TOP_K):
        m = jnp.max(work, axis=-1, keepdims=True)
        ix = jnp.min(jnp.where(work == m, lane, float(N_EXPERTS)), axis=-1, keepdims=True)
        sel = lane == ix
        work = jnp.where(sel, -jnp.inf, work)
        chosen = jnp.where(sel, 1.0, chosen)
        vals.append(m)
        idxs.append(ix)
    ev = [jnp.exp(v - vals[0]) for v in vals]
    den = ev[0] + ev[1] + ev[2] + ev[3]

    r = lax.broadcasted_iota(jnp.int32, (tm, tm), 0)
    c = lax.broadcasted_iota(jnp.int32, (tm, tm), 1)
    before = jnp.where(r > c, 1.0, 0.0).astype(BF16)
    running = _dot(before, chosen.astype(BF16)) + carry_ref[...]
    ranks = [jnp.sum(jnp.where(lane == ix, running, 0.0), axis=-1, keepdims=True) for ix in idxs]
    carry_ref[...] = carry_ref[...] + jnp.sum(chosen, axis=0, keepdims=True)
    cnt_ref[...] = carry_ref[...]

    l4 = lax.broadcasted_iota(jnp.int32, (tm, TOP_K), 1)

    def cols(xs):
        return jnp.where(l4 == 0, xs[0], jnp.where(l4 == 1, xs[1], jnp.where(l4 == 2, xs[2], xs[3])))

    idx_ref[...] = cols(idxs).astype(jnp.int32)
    rank_ref[...] = cols(ranks).astype(jnp.int32)
    gate_ref[...] = cols([e / den for e in ev])


def _post_mixer(x, ya, yb, w_out, ln_g, ln_b, wq, k_mem, v_mem, wo, w_router, b_router):
    B, S, D = x.shape
    M = k_mem.shape[1]
    full = lambda a: pl.BlockSpec(a.shape, lambda b, i: (0,) * a.ndim)
    tile = lambda w: pl.BlockSpec((None, TM_POST, w), lambda b, i: (b, i, 0))
    memb = pl.BlockSpec((None, M, D), lambda b, i: (b, 0, 0))
    sds = jax.ShapeDtypeStruct
    return pl.pallas_call(
        _post_mixer_kernel,
        out_shape=(sds((B, S, D), F32), sds((B, S, D // 2), jnp.uint32),
                   sds((B, S, TOP_K), jnp.int32), sds((B, S, TOP_K), jnp.int32),
                   sds((B, S, TOP_K), F32), sds((1, N_EXPERTS), F32)),
        grid=(B, S // TM_POST),
        in_specs=[tile(D), tile(SGU_WIDTH), tile(SB_WIDTH), full(w_out), full(ln_g), full(ln_b),
                  full(wq), memb, memb, full(wo), full(w_router), full(b_router)],
        out_specs=(tile(D), tile(D // 2), tile(TOP_K), tile(TOP_K), tile(TOP_K),
                   pl.BlockSpec((1, N_EXPERTS), lambda b, i: (0, 0))),
        scratch_shapes=[pltpu.VMEM((1, N_EXPERTS), F32)],
        compiler_params=pltpu.CompilerParams(
            dimension_semantics=("arbitrary", "arbitrary"), vmem_limit_bytes=VMEM_LIMIT),
        name="post_mixer",
    )(x, ya, yb, w_out, ln_g, ln_b, wq, k_mem, v_mem, wo, w_router, b_router)


def _dest_rows_kernel(pstart_ref, idx_ref, rank_ref, dest_ref):
    idx = idx_ref[...]
    base = jnp.zeros_like(idx)
    for e in range(N_EXPERTS):
        base = jnp.where(idx == e, pstart_ref[e], base)
    dest_ref[...] = base + rank_ref[...]


def _dest_rows(pstart, idx_flat, rank_flat):
    n = idx_flat.shape[0]
    shape2d = (n // LANES, LANES)
    full = pl.BlockSpec(shape2d, lambda i, ps: (0, 0))
    dest = pl.pallas_call(
        _dest_rows_kernel,
        out_shape=jax.ShapeDtypeStruct(shape2d, jnp.int32),
        grid_spec=pltpu.PrefetchScalarGridSpec(
            num_scalar_prefetch=1, grid=(1,), in_specs=[full, full], out_specs=full),
        name="dest_rows",
    )(pstart, idx_flat.reshape(shape2d), rank_flat.reshape(shape2d))
    return dest.reshape(n)


def _dispatch(dest_by_choice, x2p, n_rows):
    N, W = x2p.shape
    workers = SC_CORES * SC_SUBCORES
    per_worker = N // workers
    mesh = plsc.VectorSubcoreMesh(core_axis_name="core", subcore_axis_name="subcore",
                                  num_cores=SC_CORES, num_subcores=SC_SUBCORES)

    @functools.partial(
        pl.kernel, mesh=mesh,
        out_type=jax.ShapeDtypeStruct((n_rows, W), jnp.uint32),
        scratch_types=[pltpu.VMEM((SC_CHUNK,), jnp.int32), pltpu.VMEM((SC_CHUNK, W), jnp.uint32)],
        name="dispatch",
    )
    def scatter(x_hbm, dest_hbm, rows_hbm, idx_v, rows_v):
        worker = lax.axis_index("subcore") * SC_CORES + lax.axis_index("core")

        @pl.loop(0, per_worker // SC_CHUNK)
        def _(c):
            base = worker * per_worker + c * SC_CHUNK
            pltpu.sync_copy(x_hbm.at[pl.ds(base, SC_CHUNK)], rows_v)
            for k in range(TOP_K):
                pltpu.sync_copy(dest_hbm.at[k, pl.ds(base, SC_CHUNK)], idx_v)
                pltpu.sync_copy(rows_v, rows_hbm.at[idx_v])

    return scatter(x2p, dest_by_choice)


def _experts_kernel(layer, be_ref, first_ref, nvalid_ref, next_ref, slot_ref, used_ref,
                    rows_ref, bgu_ref, bd_ref, wgu_hbm, wd_hbm,
                    y_ref, wgu_f32, wd_f32, wgu_bf, wd_bf, sem):
    i = pl.program_id(0)

    def fetch(e, slot):
        return (pltpu.make_async_copy(wgu_hbm.at[layer, e], wgu_f32.at[slot], sem.at[0, slot]),
                pltpu.make_async_copy(wd_hbm.at[layer, e], wd_f32.at[slot], sem.at[1, slot]))

    @pl.when(i < nvalid_ref[0])
    def _():
        @pl.when(first_ref[i] == 1)
        def _():
            slot = slot_ref[i]

            @pl.when(i == 0)
            def _():
                for c in fetch(be_ref[i], slot):
                    c.start()

            for c in fetch(be_ref[i], slot):
                c.wait()
            wgu_bf[...] = wgu_f32[slot].astype(BF16)
            wd_bf[...] = wd_f32[slot].astype(BF16)

            @pl.when(next_ref[i] >= 0)
            def _():
                for c in fetch(next_ref[i], 1 - slot):
                    c.start()

        live = lax.broadcasted_iota(jnp.int32, rows_ref.shape, 0) < used_ref[i]
        lo, hi = _unpack_bf16_pairs(jnp.where(live, rows_ref[...], jnp.uint32(0)))
        half = D_MODEL // 2
        hgu = _dot(lo, wgu_bf[:half, :]) + _dot(hi, wgu_bf[half:, :]) + bgu_ref[...]
        g = jnp.minimum(hgu[:, :D_MODEL], SWIGLU_LIMIT)
        u = jnp.clip(hgu[:, D_MODEL:], -SWIGLU_LIMIT, SWIGLU_LIMIT)
        a = (u + 1.0) * (g * jax.nn.sigmoid(SWIGLU_ALPHA * g))
        y_ref[...] = _pack_bf16_pairs(_dot(a.astype(BF16), wd_bf[...]) + bd_ref[...])

    @pl.when(i >= nvalid_ref[0])
    def _():
        y_ref[...] = jnp.zeros_like(y_ref)


def _experts(layer, plan, rows, w_gu, b_gu, w_down, b_down):
    P, W = rows.shape
    L, E, D, F2 = w_gu.shape
    nb = P // BLK_E

    def row_map(i, be, bf, nv, nx, sl, us):
        return (jnp.minimum(i, nv[0] - 1), 0)

    def b_map(i, be, bf, nv, nx, sl, us):
        return (layer, be[i], 0, 0)

    grid_spec = pltpu.PrefetchScalarGridSpec(
        num_scalar_prefetch=6,
        grid=(nb,),
        in_specs=[pl.BlockSpec((BLK_E, W), row_map),
                  pl.BlockSpec((None, None, 1, F2), b_map),
                  pl.BlockSpec((None, None, 1, D), b_map),
                  pl.BlockSpec(memory_space=pl.ANY),
                  pl.BlockSpec(memory_space=pl.ANY)],
        out_specs=pl.BlockSpec((BLK_E, D // 2), lambda i, *_: (i, 0)),
        scratch_shapes=[pltpu.VMEM((2, D, F2), F32), pltpu.VMEM((2, F2 // 2, D), F32),
                        pltpu.VMEM((D, F2), BF16), pltpu.VMEM((F2 // 2, D), BF16),
                        pltpu.SemaphoreType.DMA((2, 2))],
    )
    return pl.pallas_call(
        functools.partial(_experts_kernel, layer),
        out_shape=jax.ShapeDtypeStruct((P, D // 2), jnp.uint32),
        grid_spec=grid_spec,
        compiler_params=pltpu.CompilerParams(
            dimension_semantics=("arbitrary",), vmem_limit_bytes=VMEM_LIMIT),
        name="experts",
    )(plan.block_e, plan.block_first, plan.nvalid, plan.next_e, plan.slot, plan.block_rows, rows,
      b_gu.reshape(L, E, 1, F2), b_down.reshape(L, E, 1, D), w_gu, w_down)


def _gather_expert_rows(dest_by_choice, y):
    n_choice, N = dest_by_choice.shape
    D = y.shape[1]
    workers = SC_CORES * SC_SUBCORES
    per_worker = N // workers
    mesh = plsc.VectorSubcoreMesh(core_axis_name="core", subcore_axis_name="subcore",
                                  num_cores=SC_CORES, num_subcores=SC_SUBCORES)

    @functools.partial(
        pl.kernel, mesh=mesh,
        out_type=jax.ShapeDtypeStruct((n_choice, N, D), y.dtype),
        scratch_types=[pltpu.VMEM((SC_CHUNK,), jnp.int32), pltpu.VMEM((SC_CHUNK, D), y.dtype)],
        name="gather_expert_rows",
    )
    def gather(y_hbm, dest_hbm, out_hbm, idx_v, rows_v):
        worker = lax.axis_index("subcore") * SC_CORES + lax.axis_index("core")

        @pl.loop(0, per_worker // SC_CHUNK)
        def _(c):
            base = worker * per_worker + c * SC_CHUNK
            for k in range(n_choice):
                pltpu.sync_copy(dest_hbm.at[k, pl.ds(base, SC_CHUNK)], idx_v)
                pltpu.sync_copy(y_hbm.at[idx_v], rows_v)
                pltpu.sync_copy(rows_v, out_hbm.at[k, pl.ds(base, SC_CHUNK)])

    return gather(y, dest_by_choice)


def _combine_kernel(gate_ref, x2_ref, lng_ref, lnb_ref, yk_ref, o_ref):
    gates = gate_ref[...]
    halves = None
    for k in range(TOP_K):
        u = yk_ref[k]
        lo = lax.bitcast_convert_type(u << 16, F32)
        hi = lax.bitcast_convert_type(u & jnp.uint32(0xFFFF0000), F32)
        g = gates[:, k:k + 1]
        halves = (g * lo, g * hi) if halves is None else (halves[0] + g * lo, halves[1] + g * hi)
    ff = jnp.concatenate(halves, axis=1)
    alpha = (2 * 2) ** 0.25
    o_ref[...] = _layer_norm(alpha * x2_ref[...] + ff, lng_ref[2:3, :], lnb_ref[2:3, :])


def _combine(gates, x2, ln_g, ln_b, y_by_choice):
    N, D = x2.shape
    return pl.pallas_call(
        _combine_kernel,
        out_shape=jax.ShapeDtypeStruct((N, D), F32),
        grid=(N // TD_ROWS,),
        in_specs=[pl.BlockSpec((TD_ROWS, TOP_K), lambda i: (i, 0)),
                  pl.BlockSpec((TD_ROWS, D), lambda i: (i, 0)),
                  pl.BlockSpec(ln_g.shape, lambda i: (0, 0)),
                  pl.BlockSpec(ln_b.shape, lambda i: (0, 0)),
                  pl.BlockSpec((TOP_K, TD_ROWS, D // 2), lambda i: (0, i, 0))],
        out_specs=pl.BlockSpec((TD_ROWS, D), lambda i: (i, 0)),
        compiler_params=pltpu.CompilerParams(
            dimension_semantics=("parallel",), vmem_limit_bytes=VMEM_LIMIT),
        name="combine",
    )(gates, x2, ln_g, ln_b, y_by_choice)


class _Plan(NamedTuple):
    pstart: jax.Array
    bend: jax.Array
    nvalid: jax.Array
    block_e: jax.Array
    block_first: jax.Array
    next_e: jax.Array
    slot: jax.Array
    block_rows: jax.Array


def _routing_plan(counts, nb):
    counts = counts.reshape(N_EXPERTS).astype(jnp.int32)
    nblk = (counts + BLK_E - 1) // BLK_E
    bend = jnp.cumsum(nblk)
    pstart = (bend - nblk) * BLK_E
    nvalid = bend[-1:]
    blocks = jnp.minimum(jnp.arange(nb, dtype=jnp.int32), nvalid[0] - 1)
    block_e = jnp.sum((blocks[:, None] >= bend[None, :]).astype(jnp.int32), axis=1)
    block_e = jnp.minimum(block_e, N_EXPERTS - 1)
    block_first = jnp.concatenate(
        [jnp.ones((1,), jnp.int32), (block_e[1:] != block_e[:-1]).astype(jnp.int32)])
    experts = jnp.arange(N_EXPERTS, dtype=jnp.int32)
    at_or_after = lax.cummin(jnp.where(nblk > 0, experts, N_EXPERTS), reverse=True)
    after = jnp.concatenate([at_or_after[1:], jnp.full((1,), N_EXPERTS, jnp.int32)])
    next_e = jnp.where(after < N_EXPERTS, after, -1)[block_e]
    slot = (jnp.cumsum(block_first) - 1) & 1
    rows_before = (blocks - (bend - nblk)[block_e]) * BLK_E
    block_rows = jnp.clip(counts[block_e] - rows_before, 0, BLK_E)
    return _Plan(pstart, bend, nvalid, block_e, block_first, next_e.astype(jnp.int32),
                 slot.astype(jnp.int32), block_rows.astype(jnp.int32))


def _layer(layer, x, mem, w_in, sgu_g, sgu_b, w_sp, b_sp, grp_g, w_out, wq, wkv, wo,
           w_router, b_router, w_gu, b_gu, w_down, b_down, ln_g, ln_b):
    B, S, D = x.shape
    N = B * S
    row = lambda a: a.reshape(1, -1)

    ya, q, k, v = _mixer_in(x, w_in.astype(BF16), row(sgu_g), row(sgu_b), w_sp,
                            b_sp.reshape(SGU_GROUPS, CHUNK, 1), row(grp_g[:SGU_WIDTH]))
    yb = _sb_attn(q, k, v, row(grp_g[SGU_WIDTH:]))
    k_mem, v_mem = _kv_proj(mem, wkv.astype(BF16))
    x2, x2p, idx, rank, gates, counts = _post_mixer(
        x, ya, yb, w_out.astype(BF16), ln_g, ln_b, wq.astype(BF16), k_mem, v_mem,
        wo.astype(BF16), w_router, row(b_router))

    nb = (N * TOP_K) // BLK_E + N_EXPERTS
    plan = _routing_plan(counts, nb)
    dest = _dest_rows(plan.pstart, idx.reshape(-1), rank.reshape(-1))
    dest_by_choice = dest.reshape(N, TOP_K).T
    rows = _dispatch(dest_by_choice, x2p.reshape(N, D // 2), nb * BLK_E)
    y = _experts(layer, plan, rows, w_gu, b_gu, w_down, b_down)
    y_by_choice = _gather_expert_rows(dest_by_choice, y)
    out = _combine(gates.reshape(N, TOP_K), x2.reshape(N, D), ln_g, ln_b, y_by_choice)
    return out.reshape(B, S, D)


def kernel(x, mem, w_in, sgu_g, sgu_b, w_sp, b_sp, grp_g, w_out, wq_mem, wkv_mem, wo_mem,
           w_router, b_router, w_gu, b_gu, w_down, b_down, ln_g, ln_b):
    for l in range(w_in.shape[0]):
        x = _layer(l, x, mem, w_in[l], sgu_g[l], sgu_b[l], w_sp[l], b_sp[l], grp_g[l], w_out[l],
                   wq_mem[l], wkv_mem[l], wo_mem[l], w_router[l], b_router[l], w_gu, b_gu,
                   w_down, b_down, ln_g[l], ln_b[l])
    return x
```

```python
import functools
from typing import NamedTuple

import jax
import jax.numpy as jnp
from jax import lax
from jax.experimental import pallas as pl
from jax.experimental.pallas import tpu as pltpu
from jax.experimental.pallas import tpu_sc as plsc

F32 = jnp.float32
BF16 = jnp.bfloat16

D_MODEL = 1024
SGU_WIDTH = 512
SGU_GROUPS = 4
GROUP_DIM = 128
CHUNK = 128
SB_WIDTH = 512
SB_HEADS = 8
SB_HEAD_DIM = 64
MEM_HEADS = 4
MEM_HEAD_DIM = 256
N_EXPERTS = 32
TOP_K = 4
SWIGLU_LIMIT = 7.0
SWIGLU_ALPHA = 1.702
LN_EPS = 1e-5
RMS_EPS = 1e-6
LOG2E = 1.4426950408889634

LANES = 128
TM_MIX = 1024
TQ_SB = 512
TM_POST = 1024
TD_ROWS = 256
BLK_E = 256
SC_CORES = 2
SC_SUBCORES = 16
SC_CHUNK = 128
SC_GATHER_CHUNK = 64
VMEM_LIMIT = 56 * 1024 * 1024


def _layer_norm(x, g, b):
    mu = jnp.mean(x, axis=-1, keepdims=True)
    xc = x - mu
    var = jnp.mean(xc * xc, axis=-1, keepdims=True)
    return xc * lax.rsqrt(var + LN_EPS) * g + b


def _dot(a, b):
    return jnp.dot(a, b, preferred_element_type=F32)


def _dot_nt(a, b):
    return lax.dot_general(a, b, (((1,), (1,)), ((), ())), preferred_element_type=F32)


def _mixer_in_kernel(x_ref, win_ref, sg_ref, sb_ref, wsp_ref, bsp_ref, gg_ref,
                     ya_ref, q_ref, k_ref, v_ref):
    xb = x_ref[...].astype(BF16)

    def proj(lo, hi):
        return _dot(xb, win_ref[:, lo:hi])

    u = jax.nn.gelu(proj(0, SGU_WIDTH))
    vs = jax.nn.gelu(proj(SGU_WIDTH, 2 * SGU_WIDTH))
    r = lax.broadcasted_iota(jnp.int32, (CHUNK, CHUNK), 0)
    c = lax.broadcasted_iota(jnp.int32, (CHUNK, CHUNK), 1)
    causal = r >= c
    for g in range(SGU_GROUPS):
        gs = slice(g * GROUP_DIM, (g + 1) * GROUP_DIM)
        vn = _layer_norm(vs[:, gs], sg_ref[:, gs], sb_ref[:, gs]).astype(BF16)
        wc = jnp.where(causal, wsp_ref[g], 0.0).astype(BF16)
        for ch in range(TM_MIX // CHUNK):
            cs = slice(ch * CHUNK, (ch + 1) * CHUNK)
            gate = _dot(wc, vn[cs]) + bsp_ref[g]
            oa = u[cs, gs] * gate
            ms = jnp.mean(oa * oa, axis=-1, keepdims=True)
            ya_ref[cs, gs] = (oa * lax.rsqrt(ms + RMS_EPS) * gg_ref[:, gs]).astype(BF16)
    o = 2 * SGU_WIDTH
    q_ref[...] = (proj(o, o + SB_WIDTH) * (SB_HEAD_DIM ** -0.5)).astype(BF16)
    k_ref[...] = proj(o + SB_WIDTH, o + 2 * SB_WIDTH).astype(BF16)
    v_ref[...] = proj(o + 2 * SB_WIDTH, o + 3 * SB_WIDTH).astype(BF16)


def _mixer_in(x, w_in, sgu_g, sgu_b, w_sp, b_sp, grp_g_a):
    B, S, D = x.shape
    full = lambda shape: pl.BlockSpec(shape, lambda b, i: (0,) * len(shape))
    tile = lambda w: pl.BlockSpec((None, TM_MIX, w), lambda b, i: (b, i, 0))
    out = jax.ShapeDtypeStruct((B, S, SGU_WIDTH), BF16)
    return pl.pallas_call(
        _mixer_in_kernel,
        out_shape=(out, out, out, out),
        grid=(B, S // TM_MIX),
        in_specs=[tile(D), full(w_in.shape), full(sgu_g.shape), full(sgu_b.shape),
                  full(w_sp.shape), full(b_sp.shape), full(grp_g_a.shape)],
        out_specs=(tile(SGU_WIDTH),) * 4,
        compiler_params=pltpu.CompilerParams(
            dimension_semantics=("parallel", "parallel"), vmem_limit_bytes=VMEM_LIMIT),
        name="mixer_in",
    )(x, w_in, sgu_g, sgu_b, w_sp, b_sp, grp_g_a)


def _sb_attn_kernel(q_ref, k_ref, v_ref, gg_ref, o_ref):
    i = pl.program_id(2)
    lo = i * TQ_SB
    lane = lax.broadcasted_iota(jnp.int32, (1, LANES), 1)
    head0 = lane < SB_HEAD_DIM
    q_all = q_ref[...]
    r = lax.broadcasted_iota(jnp.int32, (2 * LANES, 2 * LANES), 0)
    c = lax.broadcasted_iota(jnp.int32, (2 * LANES, 2 * LANES), 1)
    after = jnp.where((r > c) & ((r // LANES) == (c // LANES)), 1.0, 0.0).astype(BF16)

    def split_heads(x):
        zx = jnp.zeros_like(x)
        return jnp.concatenate([jnp.where(head0, x, zx), jnp.where(head0, zx, x)], axis=0)

    earlier = ((lax.broadcasted_iota(jnp.int32, (LANES, 2 * LANES), 1) & (LANES - 1))
               < lax.broadcasted_iota(jnp.int32, (LANES, 2 * LANES), 0))

    def keep_earlier_keys(x):
        top = jnp.where(earlier, x[:LANES], 0.0)
        return jnp.concatenate([top, x[LANES:]], axis=0) if x.shape[0] > LANES else top

    def add_rows(x, r0, upd):
        new = x[r0:] + upd
        return jnp.concatenate([x[:r0], new], axis=0) if r0 else new

    def key_blocks(blocks, carry, acc):
        starts = [pl.multiple_of(s0, LANES) for _, s0, _ in blocks]
        k2s = [split_heads(k_ref[pl.ds(s, LANES), :]) for s in starts]
        v2s = [split_heads(v_ref[pl.ds(s, LANES), :]) for s in starts]
        zs = [_dot_nt(q_all[r0:], k2) for (r0, _, _), k2 in zip(blocks, k2s)]
        sps = []
        for (_, _, causal), z in zip(blocks, zs):
            sp = jnp.maximum(z, 0.0) + jnp.log(1.0 + jnp.exp2(jnp.abs(z) * (-LOG2E)))
            if causal:
                sp = keep_earlier_keys(sp)
            sps.append(sp)
        his = [sp.astype(BF16) for sp in sps]
        los = [(sp - hi.astype(F32)).astype(BF16) for sp, hi in zip(sps, his)]
        laters = [_dot(hi, after) + _dot(lo_part, after) for hi, lo_part in zip(his, los)]
        weights = []
        for (r0, _, causal), z, sp, later in zip(blocks, zs, sps, laters):
            a = jnp.exp(z - (sp + later + carry[r0:]))
            if causal:
                a = keep_earlier_keys(a)
            weights.append(a.astype(BF16))
            rows = [jnp.broadcast_to(jnp.sum(sp[:, h * LANES:(h + 1) * LANES], axis=1, keepdims=True),
                                     (z.shape[0], LANES)) for h in range(2)]
            carry = add_rows(carry, r0, jnp.concatenate(rows, axis=1))
        for (r0, _, _), a, v2 in zip(blocks, weights, v2s):
            acc = add_rows(acc, r0, _dot(a, v2))
        return carry, acc

    nsub = TQ_SB // LANES
    carry = jnp.zeros((TQ_SB, 2 * LANES), F32)
    acc = jnp.zeros((TQ_SB, LANES), F32)
    carry, acc = key_blocks([(sb * LANES, lo + sb * LANES, True) for sb in reversed(range(nsub))],
                            carry, acc)

    def body(j, st):
        s0 = pl.multiple_of(lo - (j + 1) * TQ_SB, TQ_SB)
        return key_blocks([(0, s0 + sb * LANES, False) for sb in reversed(range(nsub))], *st)

    carry, acc = lax.fori_loop(0, i, body, (carry, acc))

    sq = acc * acc
    ss0 = jnp.sum(jnp.where(head0, sq, 0.0), axis=1, keepdims=True)
    ss1 = jnp.sum(jnp.where(head0, 0.0, sq), axis=1, keepdims=True)
    inv = 1.0 / SB_HEAD_DIM
    rs = jnp.where(head0, lax.rsqrt(ss0 * inv + RMS_EPS), lax.rsqrt(ss1 * inv + RMS_EPS))
    o_ref[...] = (acc * rs * gg_ref[...]).astype(BF16)


def _sb_attn(q, k, v, grp_g_b):
    B, S, W = q.shape
    pairs = W // LANES
    return pl.pallas_call(
        _sb_attn_kernel,
        out_shape=jax.ShapeDtypeStruct((B, S, W), BF16),
        grid=(B, pairs, S // TQ_SB),
        in_specs=[pl.BlockSpec((None, TQ_SB, LANES), lambda b, p, i: (b, i, p)),
                  pl.BlockSpec((None, S, LANES), lambda b, p, i: (b, 0, p)),
                  pl.BlockSpec((None, S, LANES), lambda b, p, i: (b, 0, p)),
                  pl.BlockSpec((1, LANES), lambda b, p, i: (0, p))],
        out_specs=pl.BlockSpec((None, TQ_SB, LANES), lambda b, p, i: (b, i, p)),
        compiler_params=pltpu.CompilerParams(
            dimension_semantics=("parallel", "parallel", "arbitrary"),
            vmem_limit_bytes=VMEM_LIMIT),
        name="sb_attn",
    )(q, k, v, grp_g_b)


def _kv_proj_kernel(mem_ref, wkv_ref, k_ref, v_ref):
    mb = mem_ref[...].astype(BF16)
    k_ref[...] = _dot(mb, wkv_ref[:, :D_MODEL]).astype(BF16)
    v_ref[...] = _dot(mb, wkv_ref[:, D_MODEL:]).astype(BF16)


def _kv_proj(mem, wkv):
    B, M, D = mem.shape
    out = jax.ShapeDtypeStruct((B, M, D), BF16)
    return pl.pallas_call(
        _kv_proj_kernel,
        out_shape=(out, out),
        grid=(B,),
        in_specs=[pl.BlockSpec((None, M, D), lambda b: (b, 0, 0)),
                  pl.BlockSpec(wkv.shape, lambda b: (0, 0))],
        out_specs=(pl.BlockSpec((None, M, D), lambda b: (b, 0, 0)),) * 2,
        compiler_params=pltpu.CompilerParams(
            dimension_semantics=("parallel",), vmem_limit_bytes=VMEM_LIMIT),
        name="kv_proj",
    )(mem, wkv)


def _pack_bf16_pairs(x):
    w = x.shape[1] // 2
    bits = lax.bitcast_convert_type(x.astype(BF16).astype(F32), jnp.uint32)
    return (bits[:, w:] & jnp.uint32(0xFFFF0000)) | (bits[:, :w] >> 16)


def _unpack_bf16_pairs(u):
    lo = lax.bitcast_convert_type(u << 16, F32).astype(BF16)
    hi = lax.bitcast_convert_type(u & jnp.uint32(0xFFFF0000), F32).astype(BF16)
    return lo, hi


def _post_mixer_kernel(x_ref, ya_ref, yb_ref, wout_ref, lng_ref, lnb_ref, wq_ref, km_ref, vm_ref,
                       wo_ref, wr_ref, br_ref,
                       x2_ref, x2p_ref, idx_ref, rank_ref, gate_ref, cnt_ref, carry_ref):
    first = jnp.logical_and(pl.program_id(0) == 0, pl.program_id(1) == 0)

    @pl.when(first)
    def _():
        carry_ref[...] = jnp.zeros_like(carry_ref)

    alpha = (2 * 2) ** 0.25
    mix = _dot(ya_ref[...], wout_ref[:SGU_WIDTH, :]) + _dot(yb_ref[...], wout_ref[SGU_WIDTH:, :])
    x1 = _layer_norm(alpha * x_ref[...] + mix, lng_ref[0:1, :], lnb_ref[0:1, :])

    qm = (_dot(x1.astype(BF16), wq_ref[...]) * (MEM_HEAD_DIM ** -0.5)).astype(BF16)
    heads = []
    for h in range(MEM_HEADS):
        hs = slice(h * MEM_HEAD_DIM, (h + 1) * MEM_HEAD_DIM)
        s = _dot_nt(qm[:, hs], km_ref[:, hs])
        e = jnp.exp(s - jnp.max(s, axis=-1, keepdims=True))
        p = (e / jnp.sum(e, axis=-1, keepdims=True)).astype(BF16)
        heads.append(_dot(p, vm_ref[:, hs]).astype(BF16))
    o = jnp.concatenate(heads, axis=1)
    x2 = _layer_norm(alpha * x1 + _dot(o, wo_ref[...]), lng_ref[1:2, :], lnb_ref[1:2, :])
    x2_ref[...] = x2
    x2p_ref[...] = _pack_bf16_pairs(x2)

    xh = x2.astype(BF16)
    xl = (x2 - xh.astype(F32)).astype(BF16)
    wr = wr_ref[...]
    wh = wr.astype(BF16)
    wl = (wr - wh.astype(F32)).astype(BF16)
    logits = _dot(xh, wh) + (_dot(xl, wh) + _dot(xh, wl)) + br_ref[...]

    tm = logits.shape[0]
    lane = lax.broadcasted_iota(jnp.int32, (tm, N_EXPERTS), 1).astype(F32)
    work = logits
    vals, idxs = [], []
    chosen = jnp.zeros((tm, N_EXPERTS), F32)
    for _ in range(TOP_K):
        m = jnp.max(work, axis=-1, keepdims=True)
        ix = jnp.min(jnp.where(work == m, lane, float(N_EXPERTS)), axis=-1, keepdims=True)
        sel = lane == ix
        work = jnp.where(sel, -jnp.inf, work)
        chosen = jnp.where(sel, 1.0, chosen)
        vals.append(m)
        idxs.append(ix)
    ev = [jnp.exp(v - vals[0]) for v in vals]
    den = ev[0] + ev[1] + ev[2] + ev[3]

    r = lax.broadcasted_iota(jnp.int32, (tm, tm), 0)
    c = lax.broadcasted_iota(jnp.int32, (tm, tm), 1)
    before = jnp.where(r > c, 1.0, 0.0).astype(BF16)
    running = _dot(before, chosen.astype(BF16)) + carry_ref[...]
    ranks = [jnp.sum(jnp.where(lane == ix, running, 0.0), axis=-1, keepdims=True) for ix in idxs]
    carry_ref[...] = carry_ref[...] + jnp.sum(chosen, axis=0, keepdims=True)
    cnt_ref[...] = carry_ref[...]

    l4 = lax.broadcasted_iota(jnp.int32, (tm, TOP_K), 1)

    def cols(xs):
        return jnp.where(l4 == 0, xs[0], jnp.where(l4 == 1, xs[1], jnp.where(l4 == 2, xs[2], xs[3])))

    idx_ref[...] = cols(idxs).astype(jnp.int32)
    rank_ref[...] = cols(ranks).astype(jnp.int32)
    gate_ref[...] = cols([e / den for e in ev])


def _post_mixer(x, ya, yb, w_out, ln_g, ln_b, wq, k_mem, v_mem, wo, w_router, b_router):
    B, S, D = x.shape
    M = k_mem.shape[1]
    full = lambda a: pl.BlockSpec(a.shape, lambda b, i: (0,) * a.ndim)
    tile = lambda w: pl.BlockSpec((None, TM_POST, w), lambda b, i: (b, i, 0))
    memb = pl.BlockSpec((None, M, D), lambda b, i: (b, 0, 0))
    sds = jax.ShapeDtypeStruct
    return pl.pallas_call(
        _post_mixer_kernel,
        out_shape=(sds((B, S, D), F32), sds((B, S, D // 2), jnp.uint32),
                   sds((B, S, TOP_K), jnp.int32), sds((B, S, TOP_K), jnp.int32),
                   sds((B, S, TOP_K), F32), sds((1, N_EXPERTS), F32)),
        grid=(B, S // TM_POST),
        in_specs=[tile(D), tile(SGU_WIDTH), tile(SB_WIDTH), full(w_out), full(ln_g), full(ln_b),
                  full(wq), memb, memb, full(wo), full(w_router), full(b_router)],
        out_specs=(tile(D), tile(D // 2), tile(TOP_K), tile(TOP_K), tile(TOP_K),
                   pl.BlockSpec((1, N_EXPERTS), lambda b, i: (0, 0))),
        scratch_shapes=[pltpu.VMEM((1, N_EXPERTS), F32)],
        compiler_params=pltpu.CompilerParams(
            dimension_semantics=("arbitrary", "arbitrary"), vmem_limit_bytes=VMEM_LIMIT),
        name="post_mixer",
    )(x, ya, yb, w_out, ln_g, ln_b, wq, k_mem, v_mem, wo, w_router, b_router)


def _dest_rows_kernel(pstart_ref, idx_ref, rank_ref, dest_ref):
    idx = idx_ref[...]
    base = jnp.zeros_like(idx)
    for e in range(N_EXPERTS):
        base = jnp.where(idx == e, pstart_ref[e], base)
    dest_ref[...] = base + rank_ref[...]


def _dest_rows(pstart, idx_flat, rank_flat):
    n = idx_flat.shape[0]
    shape2d = (n // LANES, LANES)
    full = pl.BlockSpec(shape2d, lambda i, ps: (0, 0))
    dest = pl.pallas_call(
        _dest_rows_kernel,
        out_shape=jax.ShapeDtypeStruct(shape2d, jnp.int32),
        grid_spec=pltpu.PrefetchScalarGridSpec(
            num_scalar_prefetch=1, grid=(1,), in_specs=[full, full], out_specs=full),
        name="dest_rows",
    )(pstart, idx_flat.reshape(shape2d), rank_flat.reshape(shape2d))
    return dest.reshape(n)


def _dispatch(dest_by_choice, x2p, n_rows):
    N, W = x2p.shape
    workers = SC_CORES * SC_SUBCORES
    per_worker = N // workers
    mesh = plsc.VectorSubcoreMesh(core_axis_name="core", subcore_axis_name="subcore",
                                  num_cores=SC_CORES, num_subcores=SC_SUBCORES)

    @functools.partial(
        pl.kernel, mesh=mesh,
        out_type=jax.ShapeDtypeStruct((n_rows, W), jnp.uint32),
        scratch_types=[pltpu.VMEM((SC_CHUNK,), jnp.int32), pltpu.VMEM((SC_CHUNK, W), jnp.uint32)],
        name="dispatch",
    )
    def scatter(x_hbm, dest_hbm, rows_hbm, idx_v, rows_v):
        worker = lax.axis_index("subcore") * SC_CORES + lax.axis_index("core")

        @pl.loop(0, per_worker // SC_CHUNK)
        def _(c):
            base = worker * per_worker + c * SC_CHUNK
            pltpu.sync_copy(x_hbm.at[pl.ds(base, SC_CHUNK)], rows_v)
            for k in range(TOP_K):
                pltpu.sync_copy(dest_hbm.at[k, pl.ds(base, SC_CHUNK)], idx_v)
                pltpu.sync_copy(rows_v, rows_hbm.at[idx_v])

    return scatter(x2p, dest_by_choice)


def _experts_kernel(layer, be_ref, first_ref, nvalid_ref, next_ref, slot_ref, used_ref,
                    rows_ref, bgu_ref, bd_ref, wgu_hbm, wd_hbm,
                    y_ref, wgu_f32, wd_f32, wgu_bf, wd_bf, sem):
    i = pl.program_id(0)

    def fetch(e, slot):
        return (pltpu.make_async_copy(wgu_hbm.at[layer, e], wgu_f32.at[slot], sem.at[0, slot]),
                pltpu.make_async_copy(wd_hbm.at[layer, e], wd_f32.at[slot], sem.at[1, slot]))

    @pl.when(i < nvalid_ref[0])
    def _():
        @pl.when(first_ref[i] == 1)
        def _():
            slot = slot_ref[i]

            @pl.when(i == 0)
            def _():
                for c in fetch(be_ref[i], slot):
                    c.start()

            for c in fetch(be_ref[i], slot):
                c.wait()
            wgu_bf[...] = wgu_f32[slot].astype(BF16)
            wd_bf[...] = wd_f32[slot].astype(BF16)

            @pl.when(next_ref[i] >= 0)
            def _():
                for c in fetch(next_ref[i], 1 - slot):
                    c.start()

        live = lax.broadcasted_iota(jnp.int32, rows_ref.shape, 0) < used_ref[i]
        lo, hi = _unpack_bf16_pairs(jnp.where(live, rows_ref[...], jnp.uint32(0)))
        half = D_MODEL // 2
        hgu = _dot(lo, wgu_bf[:half, :]) + _dot(hi, wgu_bf[half:, :]) + bgu_ref[...]
        g = jnp.minimum(hgu[:, :D_MODEL], SWIGLU_LIMIT)
        u = jnp.clip(hgu[:, D_MODEL:], -SWIGLU_LIMIT, SWIGLU_LIMIT)
        a = (u + 1.0) * (g * jax.nn.sigmoid(SWIGLU_ALPHA * g))
        y_ref[...] = _pack_bf16_pairs(_dot(a.astype(BF16), wd_bf[...]) + bd_ref[...])

    @pl.when(i >= nvalid_ref[0])
    def _():
        y_ref[...] = jnp.zeros_like(y_ref)


def _experts(layer, plan, rows, w_gu, b_gu, w_down, b_down):
    P, W = rows.shape
    L, E, D, F2 = w_gu.shape
    nb = P // BLK_E

    def row_map(i, be, bf, nv, nx, sl, us):
        return (jnp.minimum(i, nv[0] - 1), 0)

    def b_map(i, be, bf, nv, nx, sl, us):
        return (layer, be[i], 0, 0)

    grid_spec = pltpu.PrefetchScalarGridSpec(
        num_scalar_prefetch=6,
        grid=(nb,),
        in_specs=[pl.BlockSpec((BLK_E, W), row_map),
                  pl.BlockSpec((None, None, 1, F2), b_map),
                  pl.BlockSpec((None, None, 1, D), b_map),
                  pl.BlockSpec(memory_space=pl.ANY),
                  pl.BlockSpec(memory_space=pl.ANY)],
        out_specs=pl.BlockSpec((BLK_E, D // 2), lambda i, *_: (i, 0)),
        scratch_shapes=[pltpu.VMEM((2, D, F2), F32), pltpu.VMEM((2, F2 // 2, D), F32),
                        pltpu.VMEM((D, F2), BF16), pltpu.VMEM((F2 // 2, D), BF16),
                        pltpu.SemaphoreType.DMA((2, 2))],
    )
    return pl.pallas_call(
        functools.partial(_experts_kernel, layer),
        out_shape=jax.ShapeDtypeStruct((P, D // 2), jnp.uint32),
        grid_spec=grid_spec,
        compiler_params=pltpu.CompilerParams(
            dimension_semantics=("arbitrary",), vmem_limit_bytes=VMEM_LIMIT),
        name="experts",
    )(plan.block_e, plan.block_first, plan.nvalid, plan.next_e, plan.slot, plan.block_rows, rows,
      b_gu.reshape(L, E, 1, F2), b_down.reshape(L, E, 1, D), w_gu, w_down)


def _gather_expert_rows(dest_by_choice, y):
    n_choice, N = dest_by_choice.shape
    D = y.shape[1]
    workers = SC_CORES * SC_SUBCORES
    per_worker = N // workers
    mesh = plsc.VectorSubcoreMesh(core_axis_name="core", subcore_axis_name="subcore",
                                  num_cores=SC_CORES, num_subcores=SC_SUBCORES)

    @functools.partial(
        pl.kernel, mesh=mesh,
        out_type=jax.ShapeDtypeStruct((n_choice, N, D), y.dtype),
        scratch_types=[pltpu.VMEM((2, SC_GATHER_CHUNK), jnp.int32),
                       pltpu.VMEM((2, SC_GATHER_CHUNK, D), y.dtype),
                       pltpu.SemaphoreType.DMA((2,)), pltpu.SemaphoreType.DMA((2,))],
        name="gather_expert_rows",
    )
    def gather(y_hbm, dest_hbm, out_hbm, idx_v, rows_v, gsem, wsem):
        worker = lax.axis_index("subcore") * SC_CORES + lax.axis_index("core")
        ch = SC_GATHER_CHUNK

        def write_back(k, base):
            b = k % 2
            return pltpu.make_async_copy(rows_v.at[b], out_hbm.at[k, pl.ds(base, ch)], wsem.at[b])

        @pl.loop(0, per_worker // ch)
        def _(c):
            base = worker * per_worker + c * ch
            for k in range(n_choice):
                b = k % 2
                if k >= 2:
                    write_back(k - 2, base).wait()
                pltpu.sync_copy(dest_hbm.at[k, pl.ds(base, ch)], idx_v.at[b])
                pltpu.async_copy(y_hbm.at[idx_v.at[b]], rows_v.at[b], gsem.at[b]).wait()
                write_back(k, base).start()
            for k in range(n_choice - 2, n_choice):
                write_back(k, base).wait()

    return gather(y, dest_by_choice)


def _combine_kernel(gate_ref, x2_ref, lng_ref, lnb_ref, yk_ref, o_ref):
    gates = gate_ref[...]
    halves = None
    for k in range(TOP_K):
        u = yk_ref[k]
        lo = lax.bitcast_convert_type(u << 16, F32)
        hi = lax.bitcast_convert_type(u & jnp.uint32(0xFFFF0000), F32)
        g = gates[:, k:k + 1]
        halves = (g * lo, g * hi) if halves is None else (halves[0] + g * lo, halves[1] + g * hi)
    ff = jnp.concatenate(halves, axis=1)
    alpha = (2 * 2) ** 0.25
    o_ref[...] = _layer_norm(alpha * x2_ref[...] + ff, lng_ref[2:3, :], lnb_ref[2:3, :])


def _combine(gates, x2, ln_g, ln_b, y_by_choice):
    N, D = x2.shape
    return pl.pallas_call(
        _combine_kernel,
        out_shape=jax.ShapeDtypeStruct((N, D), F32),
        grid=(N // TD_ROWS,),
        in_specs=[pl.BlockSpec((TD_ROWS, TOP_K), lambda i: (i, 0)),
                  pl.BlockSpec((TD_ROWS, D), lambda i: (i, 0)),
                  pl.BlockSpec(ln_g.shape, lambda i: (0, 0)),
                  pl.BlockSpec(ln_b.shape, lambda i: (0, 0)),
                  pl.BlockSpec((TOP_K, TD_ROWS, D // 2), lambda i: (0, i, 0))],
        out_specs=pl.BlockSpec((TD_ROWS, D), lambda i: (i, 0)),
        compiler_params=pltpu.CompilerParams(
            dimension_semantics=("parallel",), vmem_limit_bytes=VMEM_LIMIT),
        name="combine",
    )(gates, x2, ln_g, ln_b, y_by_choice)


class _Plan(NamedTuple):
    pstart: jax.Array
    bend: jax.Array
    nvalid: jax.Array
    block_e: jax.Array
    block_first: jax.Array
    next_e: jax.Array
    slot: jax.Array
    block_rows: jax.Array


def _routing_plan(counts, nb):
    counts = counts.reshape(N_EXPERTS).astype(jnp.int32)
    nblk = (counts + BLK_E - 1) // BLK_E
    bend = jnp.cumsum(nblk)
    pstart = (bend - nblk) * BLK_E
    nvalid = bend[-1:]
    blocks = jnp.minimum(jnp.arange(nb, dtype=jnp.int32), nvalid[0] - 1)
    block_e = jnp.sum((blocks[:, None] >= bend[None, :]).astype(jnp.int32), axis=1)
    block_e = jnp.minimum(block_e, N_EXPERTS - 1)
    block_first = jnp.concatenate(
        [jnp.ones((1,), jnp.int32), (block_e[1:] != block_e[:-1]).astype(jnp.int32)])
    experts = jnp.arange(N_EXPERTS, dtype=jnp.int32)
    at_or_after = lax.cummin(jnp.where(nblk > 0, experts, N_EXPERTS), reverse=True)
    after = jnp.concatenate([at_or_after[1:], jnp.full((1,), N_EXPERTS, jnp.int32)])
    next_e = jnp.where(after < N_EXPERTS, after, -1)[block_e]
    slot = (jnp.cumsum(block_first) - 1) & 1
    rows_before = (blocks - (bend - nblk)[block_e]) * BLK_E
    block_rows = jnp.clip(counts[block_e] - rows_before, 0, BLK_E)
    return _Plan(pstart, bend, nvalid, block_e, block_first, next_e.astype(jnp.int32),
                 slot.astype(jnp.int32), block_rows.astype(jnp.int32))


def _layer(layer, x, mem, w_in, sgu_g, sgu_b, w_sp, b_sp, grp_g, w_out, wq, wkv, wo,
           w_router, b_router, w_gu, b_gu, w_down, b_down, ln_g, ln_b):
    B, S, D = x.shape
    N = B * S
    row = lambda a: a.reshape(1, -1)

    ya, q, k, v = _mixer_in(x, w_in.astype(BF16), row(sgu_g), row(sgu_b), w_sp,
                            b_sp.reshape(SGU_GROUPS, CHUNK, 1), row(grp_g[:SGU_WIDTH]))
    yb = _sb_attn(q, k, v, row(grp_g[SGU_WIDTH:]))
    k_mem, v_mem = _kv_proj(mem, wkv.astype(BF16))
    x2, x2p, idx, rank, gates, counts = _post_mixer(
        x, ya, yb, w_out.astype(BF16), ln_g, ln_b, wq.astype(BF16), k_mem, v_mem,
        wo.astype(BF16), w_router, row(b_router))

    nb = (N * TOP_K) // BLK_E + N_EXPERTS
    plan = _routing_plan(counts, nb)
    dest = _dest_rows(plan.pstart, idx.reshape(-1), rank.reshape(-1))
    dest_by_choice = dest.reshape(N, TOP_K).T
    rows = _dispatch(dest_by_choice, x2p.reshape(N, D // 2), nb * BLK_E)
    y = _experts(layer, plan, rows, w_gu, b_gu, w_down, b_down)
    y_by_choice = _gather_expert_rows(dest_by_choice, y)
    out = _combine(gates.reshape(N, TOP_K), x2.reshape(N, D), ln_g, ln_b, y_by_choice)
    return out.reshape(B, S, D)


def kernel(x, mem, w_in, sgu_g, sgu_b, w_sp, b_sp, grp_g, w_out, wq_mem, wkv_mem, wo_mem,
           w_router, b_router, w_gu, b_gu, w_down, b_down, ln_g, ln_b):
    for l in range(w_in.shape[0]):
        x = _layer(l, x, mem, w_in[l], sgu_g[l], sgu_b[l], w_sp[l], b_sp[l], grp_g[l], w_out[l],
                   wq_mem[l], wkv_mem[l], wo_mem[l], w_router[l], b_router[l], w_gu, b_gu,
                   w_down, b_down, ln_g[l], ln_b[l])
    return x
```
